```python
import jax, jax.numpy as jnp
from jax import lax
import numpy as np

D_MODEL = 1024
BATCH = 8
SEQ = 8192
DEPTH = 4
DEC_BATCH = 32
DEC_SEQ = 16
PAST_LEN = 1024

CHUNK = 64
N_HEADS = 16
HEAD_DIM = D_MODEL // N_HEADS
BAND_CHUNKS = 8
BAND_PAST = BAND_CHUNKS * CHUNK
BAND_LEN = BAND_PAST + CHUNK
MAX_REL = 128
CONV_WIDTH = 31
D_FF = ((8 * D_MODEL // 3 + 127) // 128) * 128
PE_DIM = 256
N_MIXERS = 2
N_CONV = (DEPTH + 1) // 2
N_ATTN = DEPTH // 2
EPS = 1e-6
NEG_INF = -1e30

kernel_name = 'streaming_conformer_hybrid_step'


def rmsnorm(x, g):
    x32 = x.astype(jnp.float32)
    y = x32 * lax.rsqrt(jnp.mean(x32 * x32, axis=-1, keepdims=True) + EPS)
    return (y * g.astype(jnp.float32)).astype(x.dtype)


def swiglu(h, w_in, w_out):
    a, b = jnp.split(h @ w_in, 2, axis=-1)
    return (jax.nn.silu(a) * b) @ w_out


def conv_module(h, buf, w_in, b_in, dw, dw_b, g, w_out, b_out):
    a, gate = jnp.split(h @ w_in + b_in, 2, axis=-1)
    glu = a * jax.nn.sigmoid(gate)
    padded = jnp.concatenate([buf.astype(glu.dtype), glu], axis=1)
    y = lax.conv_general_dilated(padded, dw[:, None, :].astype(glu.dtype), (1,), 'VALID',
                                 dimension_numbers=('NWC', 'WIO', 'NWC'),
                                 feature_group_count=D_MODEL) + dw_b
    y = jax.nn.silu(rmsnorm(y, g))
    return y @ w_out + b_out, padded[:, padded.shape[1] - (CONV_WIDTH - 1):]


def rel_bias(table, qpos, kpos):
    rel = jnp.clip(qpos[:, None] - kpos[None, :], -MAX_REL, MAX_REL) + MAX_REL
    return table[:, rel]


def band_mask(qpos, kpos):
    qc = qpos // CHUNK
    kc = kpos // CHUNK
    return ((kpos[None, :] >= 0) & (kc[None, :] <= qc[:, None])
            & (kc[None, :] >= qc[:, None] - BAND_CHUNKS))


def band_attend(q, k, v, bias, mask):
    s = jnp.einsum('bqhd,bkhd->bhqk', q, k).astype(jnp.float32) * (HEAD_DIM ** -0.5)
    s = jnp.where(mask, s + bias.astype(jnp.float32), NEG_INF)
    p = jax.nn.softmax(s, axis=-1).astype(v.dtype)
    return jnp.einsum('bhqk,bkhd->bqhd', p, v)


def attn_prompt(h, w_qkv, w_o, table):
    B, S, _ = h.shape
    nc = S // CHUNK
    qkv = (h @ w_qkv).reshape(B, S, 3, N_HEADS, HEAD_DIM)
    q, k, v = qkv[:, :, 0], qkv[:, :, 1], qkv[:, :, 2]
    pad = jnp.zeros((B, BAND_PAST, N_HEADS, HEAD_DIM), k.dtype)
    kp = jnp.concatenate([pad, k], axis=1)
    vp = jnp.concatenate([pad, v], axis=1)
    r = jnp.arange(CHUNK)
    kk = jnp.arange(BAND_LEN) - BAND_PAST
    bias = rel_bias(table, r, kk)
    q_chunks = q.reshape(B, nc, CHUNK, N_HEADS, HEAD_DIM).transpose(1, 0, 2, 3, 4)

    def one_chunk(args):
        c, q_blk = args
        start = c * CHUNK
        k_blk = lax.dynamic_slice_in_dim(kp, start, BAND_LEN, axis=1)
        v_blk = lax.dynamic_slice_in_dim(vp, start, BAND_LEN, axis=1)
        return band_attend(q_blk, k_blk, v_blk, bias, band_mask(start + r, start + kk))

    o = lax.map(one_chunk, (jnp.arange(nc), q_chunks))
    o = o.transpose(1, 0, 2, 3, 4).reshape(B, S, D_MODEL)
    keep = min(BAND_PAST, S)
    return o @ w_o, k[:, S - keep:], v[:, S - keep:]


def attn_sample(h, ck, cv, w_qkv, w_o, table):
    B, T, _ = h.shape
    W = ck.shape[1]
    qkv = (h @ w_qkv).reshape(B, T, 3, N_HEADS, HEAD_DIM)
    q, k, v = qkv[:, :, 0], qkv[:, :, 1], qkv[:, :, 2]
    k_all = jnp.concatenate([ck.astype(k.dtype), k], axis=1)
    v_all = jnp.concatenate([cv.astype(v.dtype), v], axis=1)
    qpos = PAST_LEN + jnp.arange(T)
    kpos = jnp.concatenate([PAST_LEN - W + jnp.arange(W), qpos])
    o = band_attend(q, k_all, v_all, rel_bias(table, qpos, kpos), band_mask(qpos, kpos))
    return o.reshape(B, T, D_MODEL) @ w_o, k, v


def trunk(x, p, conv_state, cache_k, cache_v, params):
    (norm_g, final_g, ffn_w_in, ffn_w_out, conv_w_in, conv_b_in, conv_dw, conv_dw_b, conv_norm_g,
     conv_w_out, conv_b_out, attn_w_qkv, attn_w_o, attn_rel_table, pe_w_proj, pe_w_gate) = params
    is_prompt = conv_state is None
    new_conv, new_k, new_v = [], [], []
    for i in range(DEPTH):
        g = norm_g[i]
        x = x + 0.5 * swiglu(rmsnorm(x, g[0]), ffn_w_in[i, 0], ffn_w_out[i, 0])
        h = rmsnorm(x, g[1])
        j = i // N_MIXERS
        if i % N_MIXERS == 0:
            buf = jnp.zeros((x.shape[0], CONV_WIDTH - 1, D_MODEL), x.dtype) if is_prompt else conv_state[j]
            y, nb = conv_module(h, buf, conv_w_in[j], conv_b_in[j], conv_dw[j], conv_dw_b[j],
                                conv_norm_g[j], conv_w_out[j], conv_b_out[j])
            new_conv.append(nb)
        else:
            if is_prompt:
                y, nk, nv = attn_prompt(h, attn_w_qkv[j], attn_w_o[j], attn_rel_table[j])
            else:
                y, nk, nv = attn_sample(h, cache_k[j], cache_v[j], attn_w_qkv[j], attn_w_o[j],
                                        attn_rel_table[j])
            new_k.append(nk)
            new_v.append(nv)
        x = x + y
        x = x + 0.5 * swiglu(rmsnorm(x, g[2]), ffn_w_in[i, 1], ffn_w_out[i, 1])
        gate = jax.nn.sigmoid(rmsnorm(x, g[3]) @ pe_w_gate[i])
        x = x + gate * (p[i] @ pe_w_proj[i])
    return rmsnorm(x, final_g), jnp.stack(new_conv), jnp.stack(new_k), jnp.stack(new_v)


def setup_inputs(seed: int = 0) -> dict:
    key = jax.random.key(seed)
    ks = jax.random.split(key, 24)
    nrm = jax.random.normal
    f32 = jnp.float32
    kv_win = min(BAND_PAST, PAST_LEN)
    return {
        'x_prompt': nrm(ks[0], (BATCH, SEQ, D_MODEL), f32),
        'x_sample': nrm(ks[1], (DEC_BATCH, DEC_SEQ, D_MODEL), f32),
        'cache_conv': 0.5 * nrm(ks[2], (N_CONV, DEC_BATCH, CONV_WIDTH - 1, D_MODEL), f32),
        'cache_k': nrm(ks[3], (N_ATTN, DEC_BATCH, kv_win, N_HEADS, HEAD_DIM), f32),
        'cache_v': nrm(ks[4], (N_ATTN, DEC_BATCH, kv_win, N_HEADS, HEAD_DIM), f32),
        'p_prompt': nrm(ks[5], (DEPTH, BATCH, SEQ, PE_DIM), f32),
        'p_sample': nrm(ks[6], (DEPTH, DEC_BATCH, DEC_SEQ, PE_DIM), f32),
        'norm_g': 1.0 + 0.01 * nrm(ks[7], (DEPTH, 4, D_MODEL), f32),
        'final_g': 1.0 + 0.01 * nrm(ks[8], (D_MODEL,), f32),
        'ffn_w_in': nrm(ks[9], (DEPTH, 2, D_MODEL, 2 * D_FF), f32) * D_MODEL ** -0.5,
        'ffn_w_out': nrm(ks[10], (DEPTH, 2, D_FF, D_MODEL), f32) * D_FF ** -0.5,
        'conv_w_in': nrm(ks[11], (N_CONV, D_MODEL, 2 * D_MODEL), f32) * D_MODEL ** -0.5,
        'conv_b_in': 0.01 * nrm(ks[12], (N_CONV, 2 * D_MODEL), f32),
        'conv_dw': nrm(ks[13], (N_CONV, CONV_WIDTH, D_MODEL), f32) * CONV_WIDTH ** -0.5,
        'conv_dw_b': 0.01 * nrm(ks[14], (N_CONV, D_MODEL), f32),
        'conv_norm_g': 1.0 + 0.01 * nrm(ks[15], (N_CONV, D_MODEL), f32),
        'conv_w_out': nrm(ks[16], (N_CONV, D_MODEL, D_MODEL), f32) * D_MODEL ** -0.5,
        'conv_b_out': 0.01 * nrm(ks[17], (N_CONV, D_MODEL), f32),
        'attn_w_qkv': nrm(ks[18], (N_ATTN, D_MODEL, 3 * D_MODEL), f32) * D_MODEL ** -0.5,
        'attn_w_o': nrm(ks[19], (N_ATTN, D_MODEL, D_MODEL), f32) * D_MODEL ** -0.5,
        'attn_rel_table': 0.1 * nrm(ks[20], (N_ATTN, N_HEADS, 2 * MAX_REL + 1), f32),
        'pe_w_proj': nrm(ks[21], (DEPTH, PE_DIM, D_MODEL), f32) * PE_DIM ** -0.5,
        'pe_w_gate': nrm(ks[22], (DEPTH, D_MODEL, D_MODEL), f32) * D_MODEL ** -0.5,
    }


def reference(x_prompt, x_sample, cache_conv, cache_k, cache_v, p_prompt, p_sample,
              norm_g, final_g, ffn_w_in, ffn_w_out, conv_w_in, conv_b_in, conv_dw, conv_dw_b,
              conv_norm_g, conv_w_out, conv_b_out, attn_w_qkv, attn_w_o, attn_rel_table,
              pe_w_proj, pe_w_gate):
    params = (norm_g, final_g, ffn_w_in, ffn_w_out, conv_w_in, conv_b_in, conv_dw, conv_dw_b,
              conv_norm_g, conv_w_out, conv_b_out, attn_w_qkv, attn_w_o, attn_rel_table,
              pe_w_proj, pe_w_gate)
    y_prompt, conv_prompt, k_prompt, v_prompt = trunk(x_prompt, p_prompt, None, None, None, params)
    y_sample, conv_sample, k_sample, v_sample = trunk(x_sample, p_sample, cache_conv, cache_k, cache_v,
                                                      params)
    return (y_prompt, y_sample, conv_prompt, k_prompt, v_prompt, conv_sample, k_sample, v_sample)
```

```python
import functools

import numpy as np
import jax
import jax.numpy as jnp
from jax import lax
from jax.experimental import pallas as pl
from jax.experimental.pallas import tpu as pltpu

CHUNK = 64
BAND_CHUNKS = 8
BAND_PAST = BAND_CHUNKS * CHUNK
MAX_REL = 128
PAST_LEN = 1024
EPS = 1e-6
NEG_INF = -1e30

LANES = 128
HEAD_PAIR = LANES
CONV_HALO = 32
Q_TILE = 4 * CHUNK
VMEM_LIMIT = 56 * 1024 * 1024

F32 = jnp.float32
BF16 = jnp.bfloat16


def _rms(x, g):
    ms = jnp.mean(x * x, axis=-1, keepdims=True)
    return x * lax.rsqrt(ms + EPS) * g


def _dot(a, b):
    return jnp.dot(a, b, preferred_element_type=F32)


def _resident(block_shape, index):
    return pl.BlockSpec(block_shape, lambda *_: index, pipeline_mode=pl.Buffered(1))


def _params(*semantics):
    return pltpu.CompilerParams(dimension_semantics=semantics, vmem_limit_bytes=VMEM_LIMIT)


def _ffn_kernel(*refs, d_ff, chunks, has_oproj, has_pe, has_final):
    it = iter(refs)
    x_ref = next(it)
    if has_oproj:
        o_ref, wo_ref = next(it), next(it)
    g_ref, win_ref, wout_ref = next(it), next(it), next(it)
    if has_pe:
        g3_ref, wgate_ref, p_ref, wproj_ref = next(it), next(it), next(it), next(it)
    if has_final:
        gf_ref = next(it)
    out_ref = next(it)

    x = x_ref[...]
    if has_oproj:
        o = jnp.concatenate([o_ref[hp] for hp in range(o_ref.shape[0])], axis=-1)
        x = x + _dot(o, wo_ref[...])
    h = _rms(x, g_ref[...]).astype(BF16)
    acc = None
    for c0, c1 in chunks:
        a = _dot(h, win_ref[:, c0:c1])
        b = _dot(h, win_ref[:, d_ff + c0:d_ff + c1])
        u = (a * jax.nn.sigmoid(a) * b).astype(BF16)
        d = _dot(u, wout_ref[c0:c1, :])
        acc = d if acc is None else acc + d
    x = x + 0.5 * acc
    if has_pe:
        gate = jax.nn.sigmoid(_dot(_rms(x, g3_ref[...]).astype(BF16), wgate_ref[...]))
        x = x + gate * _dot(p_ref[...].astype(BF16), wproj_ref[...])
    if has_final:
        x = _rms(x, gf_ref[...])
    out_ref[...] = x


def _ffn(x, g, w_in, w_out, layer, half, *, tm, oproj=None, pe=None, final_g=None):
    n, d = x.shape
    d_ff = w_out.shape[2]
    step = 512 if d_ff >= 512 else d_ff
    chunks = tuple((c, min(c + step, d_ff)) for c in range(0, d_ff, step))
    row = lambda i: (i, 0)
    args, specs = [x], [pl.BlockSpec((tm, d), row)]
    if oproj is not None:
        o, w_o, li = oproj
        args += [o, w_o]
        specs += [pl.BlockSpec((o.shape[0], tm, HEAD_PAIR), lambda i: (0, i, 0)),
                  _resident((None, d, d), (li, 0, 0))]
    args += [g, w_in, w_out]
    specs += [_resident((1, d), (0, 0)),
              _resident((None, None, d, 2 * d_ff), (layer, half, 0, 0)),
              _resident((None, None, d_ff, d), (layer, half, 0, 0))]
    if pe is not None:
        g3, w_gate, p, w_proj = pe
        pe_dim = p.shape[-1]
        args += [g3, w_gate, p, w_proj]
        specs += [_resident((1, d), (0, 0)),
                  _resident((None, d, d), (layer, 0, 0)),
                  pl.BlockSpec((None, tm, pe_dim), lambda i: (layer, i, 0)),
                  _resident((None, pe_dim, d), (layer, 0, 0))]
    if final_g is not None:
        args += [final_g]
        specs += [_resident((1, d), (0, 0))]
    body = functools.partial(_ffn_kernel, d_ff=d_ff, chunks=chunks, has_oproj=oproj is not None,
                             has_pe=pe is not None, has_final=final_g is not None)
    return pl.pallas_call(
        body, grid=(n // tm,), in_specs=specs, out_specs=pl.BlockSpec((tm, d), row),
        out_shape=jax.ShapeDtypeStruct((n, d), F32), compiler_params=_params("arbitrary"),
        name="ffn")(*args)


def _conv_kernel(x_ref, buf_ref, g1_ref, win_ref, bin_ref, dw_ref, dwb_ref, cg_ref, wout_ref, bout_ref,
                 out_ref, nbuf_ref, ext_ref, *, tm, width, row_block):
    d = x_ref.shape[-1]
    ncb = d // LANES

    @pl.when(pl.program_id(1) == 0)
    def _():
        for cb in range(ncb):
            ext_ref[cb, 0:CONV_HALO, :] = buf_ref[0, :, cb * LANES:(cb + 1) * LANES]

    x = x_ref[0]
    h = _rms(x, g1_ref[...]).astype(BF16)
    ag = _dot(h, win_ref[...]) + bin_ref[...]
    glu = ag[:, :d] * jax.nn.sigmoid(ag[:, d:])
    for cb in range(ncb):
        ext_ref[cb, CONV_HALO:CONV_HALO + tm, :] = glu[:, cb * LANES:(cb + 1) * LANES]

    lead = CONV_HALO - (width - 1)
    cols = []
    for cb in range(ncb):
        lanes = slice(cb * LANES, (cb + 1) * LANES)
        blocks = []
        for r0 in range(0, tm, row_block):
            acc = jnp.broadcast_to(dwb_ref[:, lanes], (row_block, LANES))
            for j in range(width):
                acc = acc + dw_ref[j:j + 1, lanes] * ext_ref[cb, r0 + j + lead:r0 + j + lead + row_block, :]
            blocks.append(acc)
        cols.append(jnp.concatenate(blocks, axis=0) if len(blocks) > 1 else blocks[0])
    y = jnp.concatenate(cols, axis=-1)
    y = _rms(y, cg_ref[...])
    y = (y * jax.nn.sigmoid(y)).astype(BF16)
    out_ref[0] = x + (_dot(y, wout_ref[...]) + bout_ref[...])

    for cb in range(ncb):
        tail = ext_ref[cb, tm:tm + CONV_HALO, :]
        nbuf_ref[0, :, cb * LANES:(cb + 1) * LANES] = tail
        ext_ref[cb, 0:CONV_HALO, :] = tail


def _conv(x, buf, g1, w_in, b_in, dw, dw_b, cg, w_out, b_out, j, *, tm):
    b, s, d = x.shape
    width = dw.shape[1]
    row_block = min(tm, 128)
    tile = lambda bi, ti: (bi, ti, 0)
    specs = [pl.BlockSpec((1, tm, d), tile),
             pl.BlockSpec((1, CONV_HALO, d), lambda bi, ti: (bi, 0, 0)),
             _resident((1, d), (0, 0)),
             _resident((None, d, 2 * d), (j, 0, 0)),
             _resident((None, 1, 2 * d), (j, 0, 0)),
             _resident((None, width, d), (j, 0, 0)),
             _resident((None, 1, d), (j, 0, 0)),
             _resident((None, 1, d), (j, 0, 0)),
             _resident((None, d, d), (j, 0, 0)),
             _resident((None, 1, d), (j, 0, 0))]
    body = functools.partial(_conv_kernel, tm=tm, width=width, row_block=row_block)
    return pl.pallas_call(
        body, grid=(b, s // tm), in_specs=specs,
        out_specs=[pl.BlockSpec((1, tm, d), tile), pl.BlockSpec((1, CONV_HALO, d), lambda bi, ti: (bi, 0, 0))],
        out_shape=[jax.ShapeDtypeStruct((b, s, d), F32), jax.ShapeDtypeStruct((b, CONV_HALO, d), F32)],
        scratch_shapes=[pltpu.VMEM((d // LANES, CONV_HALO + tm, LANES), F32)],
        compiler_params=_params("arbitrary", "arbitrary"), name="conv_module",
    )(x, buf, g1, w_in, b_in, dw, dw_b, cg, w_out, b_out)


def _qkv_kernel(x_ref, g_ref, w_ref, q_ref, k_ref, v_ref, k32_ref, v32_ref, *, scale):
    d = x_ref.shape[-1]
    h = _rms(x_ref[...], g_ref[...]).astype(BF16)
    qkv = _dot(h, w_ref[...])
    for hp in range(d // HEAD_PAIR):
        lanes = slice(hp * HEAD_PAIR, (hp + 1) * HEAD_PAIR)
        q_ref[hp] = (qkv[:, lanes] * scale).astype(BF16)
        k_ref[hp] = qkv[:, d:2 * d][:, lanes].astype(BF16)
        v_ref[hp] = qkv[:, 2 * d:][:, lanes].astype(BF16)
    k32_ref[...] = qkv[:, d:2 * d]
    v32_ref[...] = qkv[:, 2 * d:]


def _qkv(x, g, w_qkv, j, *, tm, tiles_per_tail, scale):
    n, d = x.shape
    nt = n // tm
    hp = d // HEAD_PAIR
    pair_spec = pl.BlockSpec((hp, tm, HEAD_PAIR), lambda i: (0, i, 0))
    tail_spec = pl.BlockSpec((tm, d), lambda i: (i // tiles_per_tail, 0))
    pair_shape = jax.ShapeDtypeStruct((hp, n, HEAD_PAIR), BF16)
    tail_shape = jax.ShapeDtypeStruct((nt // tiles_per_tail * tm, d), F32)
    return pl.pallas_call(
        functools.partial(_qkv_kernel, scale=scale), grid=(nt,),
        in_specs=[pl.BlockSpec((tm, d), lambda i: (i, 0)), _resident((1, d), (0, 0)),
                  _resident((None, d, 3 * d), (j, 0, 0))],
        out_specs=[pair_spec, pair_spec, pair_spec, tail_spec, tail_spec],
        out_shape=[pair_shape, pair_shape, pair_shape, tail_shape, tail_shape],
        compiler_params=_params("arbitrary"), name="qkv_proj")(x, g, w_qkv)


def _attend_pair(q2, keys, values, biases, head_dim):
    lane = lax.broadcasted_iota(jnp.int32, (1, HEAD_PAIR), 1)
    out = None
    for hh in range(2):
        mine = (lane < head_dim) if hh == 0 else (lane >= head_dim)
        qh = jnp.where(mine, q2, jnp.zeros_like(q2))
        s = [lax.dot_general(qh, k, (((1,), (1,)), ((), ())), preferred_element_type=F32) + bias
             for k, bias in zip(keys, biases[hh])]
        m = functools.reduce(jnp.maximum, [jnp.max(sj, axis=-1, keepdims=True) for sj in s])
        p = [jnp.exp(sj - m) for sj in s]
        l = functools.reduce(jnp.add, [jnp.sum(pj, axis=-1, keepdims=True) for pj in p])
        o = functools.reduce(jnp.add, [_dot(pj.astype(BF16), jnp.where(mine, v, jnp.zeros_like(v)))
                                       for pj, v in zip(p, values)])
        o = o / l
        out = o if out is None else out + o
    return out


def _attn_prompt_kernel(q_ref, k0_ref, k1_ref, k2_ref, v0_ref, v1_ref, v2_ref, bm_ref, o_ref, *, head_dim):
    t = pl.program_id(1)
    k_refs, v_refs = (k0_ref, k1_ref, k2_ref), (v0_ref, v1_ref, v2_ref)

    def run(blocks):
        for hp in range(q_ref.shape[0]):
            o = _attend_pair(q_ref[hp], [k_refs[j][hp] for j in blocks], [v_refs[j][hp] for j in blocks],
                             [[bm_ref[2 * hp + hh, j] for j in blocks] for hh in range(2)], head_dim)
            o_ref[hp] = o.astype(BF16)

    pl.when(t == 0)(lambda: run((2,)))
    pl.when(t == 1)(lambda: run((1, 2)))
    pl.when(t >= 2)(lambda: run((0, 1, 2)))


def _prompt_bias(table):
    r = np.arange(Q_TILE)[:, None]
    kk = np.arange(3 * Q_TILE)[None, :] - 2 * Q_TILE
    rel = np.clip(r - kk, -MAX_REL, MAX_REL) + MAX_REL
    qc, kc = r // CHUNK, kk // CHUNK
    mask = (kc <= qc) & (kc >= qc - BAND_CHUNKS)
    bias = jnp.where(mask[None], table[:, rel], NEG_INF)
    h = table.shape[0]
    return bias.reshape(h, Q_TILE, 3, Q_TILE).transpose(0, 2, 1, 3)


def _attn_prompt(q, k, v, table, b, s):
    hp, n, _ = q.shape
    nq = s // Q_TILE
    head_dim = HEAD_PAIR // 2
    bm = _prompt_bias(table)

    def key_spec(j):
        return pl.BlockSpec((hp, Q_TILE, HEAD_PAIR), lambda bi, ti: (0, bi * nq + jnp.maximum(ti - 2 + j, 0), 0))

    tile_spec = pl.BlockSpec((hp, Q_TILE, HEAD_PAIR), lambda bi, ti: (0, bi * nq + ti, 0))
    return pl.pallas_call(
        functools.partial(_attn_prompt_kernel, head_dim=head_dim), grid=(b, nq),
        in_specs=[tile_spec, key_spec(0), key_spec(1), key_spec(2), key_spec(0), key_spec(1), key_spec(2),
                  _resident(bm.shape, (0, 0, 0, 0))],
        out_specs=tile_spec, out_shape=jax.ShapeDtypeStruct((hp, n, HEAD_PAIR), BF16),
        compiler_params=_params("arbitrary", "arbitrary"), name="attn_prompt")(q, k, k, k, v, v, v, bm)


def _attn_sample_kernel(q_ref, kn_ref, vn_ref, ck_ref, cv_ref, bc_ref, bn_ref, o_ref, *, head_dim):
    for hp in range(q_ref.shape[0]):
        lanes = slice(hp * HEAD_PAIR, (hp + 1) * HEAD_PAIR)
        keys = [ck_ref[0, :, lanes].astype(BF16), kn_ref[hp]]
        values = [cv_ref[0, :, lanes].astype(BF16), vn_ref[hp]]
        biases = [[bc_ref[2 * hp + hh], bn_ref[2 * hp + hh]] for hh in range(2)]
        o_ref[hp] = _attend_pair(q_ref[hp], keys, values, biases, head_dim).astype(BF16)


def _attn_sample(q, k, v, ck, cv, table, b, t):
    hp, n, _ = q.shape
    w, d = ck.shape[1], ck.shape[2]
    head_dim = HEAD_PAIR // 2
    qpos = PAST_LEN + np.arange(t)
    kpos = np.concatenate([PAST_LEN - w + np.arange(w), qpos])
    rel = np.clip(qpos[:, None] - kpos[None, :], -MAX_REL, MAX_REL) + MAX_REL
    qc, kc = qpos[:, None] // CHUNK, kpos[None, :] // CHUNK
    mask = (kpos[None, :] >= 0) & (kc <= qc) & (kc >= qc - BAND_CHUNKS)
    bias = jnp.where(mask[None], table[:, rel], NEG_INF)
    new_spec = pl.BlockSpec((hp, t, HEAD_PAIR), lambda bi: (0, bi, 0))
    cache_spec = pl.BlockSpec((1, w, d), lambda bi: (bi, 0, 0))
    return pl.pallas_call(
        functools.partial(_attn_sample_kernel, head_dim=head_dim), grid=(b,),
        in_specs=[new_spec, new_spec, new_spec, cache_spec, cache_spec,
                  _resident((bias.shape[0], t, w), (0, 0, 0)), _resident((bias.shape[0], t, t), (0, 0, 0))],
        out_specs=new_spec, out_shape=jax.ShapeDtypeStruct((hp, n, HEAD_PAIR), BF16),
        compiler_params=_params("arbitrary"), name="attn_sample",
    )(q, k, v, ck, cv, bias[:, :, :w], bias[:, :, w:])


def _trunk(x, p, conv_state, cache_k, cache_v, w, *, tm):
    b, s, d = x.shape
    depth = w["norm_g"].shape[0]
    n = b * s
    is_prompt = conv_state is None
    n_heads, head_dim = cache_k.shape[3], cache_k.shape[4]
    width = w["conv_dw"].shape[1]
    p = p.reshape(depth, n, p.shape[-1])
    g = lambda i, k: w["norm_g"][i, k].reshape(1, d)
    x = x.reshape(n, d)
    new_conv, new_k, new_v = [], [], []
    oproj = None
    for i in range(depth):
        j = i // 2
        x = _ffn(x, g(i, 0), w["ffn_w_in"], w["ffn_w_out"], i, 0, tm=tm)
        if i % 2 == 0:
            if is_prompt:
                buf = jnp.zeros((b, CONV_HALO, d), F32)
            else:
                buf = jnp.pad(conv_state[j], ((0, 0), (CONV_HALO - (width - 1), 0), (0, 0)))
            x, nbuf = _conv(x.reshape(b, s, d), buf, g(i, 1), w["conv_w_in"], w["conv_b_in"], w["conv_dw"],
                            w["conv_dw_b"], w["conv_norm_g"], w["conv_w_out"], w["conv_b_out"], j,
                            tm=min(tm, s))
            x = x.reshape(n, d)
            new_conv.append(nbuf[:, CONV_HALO - (width - 1):])
            oproj = None
        else:
            keep = min(BAND_PAST, s) if is_prompt else s
            q, k, v, k32, v32 = _qkv(x, g(i, 1), w["attn_w_qkv"], j, tm=tm,
                                     tiles_per_tail=(s // keep) if is_prompt else 1, scale=head_dim ** -0.5)
            if is_prompt:
                o = _attn_prompt(q, k, v, w["attn_rel_table"][j], b, s)
            else:
                win = cache_k.shape[2]
                o = _attn_sample(q, k, v, cache_k[j].reshape(b, win, d), cache_v[j].reshape(b, win, d),
                                 w["attn_rel_table"][j], b, s)
            new_k.append(k32.reshape(b, keep, n_heads, head_dim))
            new_v.append(v32.reshape(b, keep, n_heads, head_dim))
            oproj = (o, w["attn_w_o"], j)
        x = _ffn(x, g(i, 2), w["ffn_w_in"], w["ffn_w_out"], i, 1, tm=tm, oproj=oproj,
                 pe=(g(i, 3), w["pe_w_gate"], p, w["pe_w_proj"]),
                 final_g=w["final_g"].reshape(1, d) if i == depth - 1 else None)
    return x.reshape(b, s, d), jnp.stack(new_conv), jnp.stack(new_k), jnp.stack(new_v)


def kernel(x_prompt, x_sample, cache_conv, cache_k, cache_v, p_prompt, p_sample, norm_g, final_g, ffn_w_in,
           ffn_w_out, conv_w_in, conv_b_in, conv_dw, conv_dw_b, conv_norm_g, conv_w_out, conv_b_out,
           attn_w_qkv, attn_w_o, attn_rel_table, pe_w_proj, pe_w_gate):
    n_conv = conv_w_in.shape[0]
    w = dict(norm_g=norm_g, final_g=final_g,
             ffn_w_in=ffn_w_in.astype(BF16), ffn_w_out=ffn_w_out.astype(BF16),
             conv_w_in=conv_w_in.astype(BF16), conv_b_in=conv_b_in.reshape(n_conv, 1, -1), conv_dw=conv_dw,
             conv_dw_b=conv_dw_b.reshape(n_conv, 1, -1), conv_norm_g=conv_norm_g.reshape(n_conv, 1, -1),
             conv_w_out=conv_w_out.astype(BF16), conv_b_out=conv_b_out.reshape(n_conv, 1, -1),
             attn_w_qkv=attn_w_qkv.astype(BF16), attn_w_o=attn_w_o.astype(BF16), attn_rel_table=attn_rel_table,
             pe_w_proj=pe_w_proj.astype(BF16), pe_w_gate=pe_w_gate.astype(BF16))
    n_sample = x_sample.shape[0] * x_sample.shape[1]
    y_p, conv_p, k_p, v_p = _trunk(x_prompt, p_prompt, None, cache_k, cache_v, w, tm=512)
    y_s, conv_s, k_s, v_s = _trunk(x_sample, p_sample, cache_conv, cache_k, cache_v, w, tm=min(512, n_sample))
    return (y_p, y_s, conv_p, k_p, v_p, conv_s, k_s, v_s)
```

```python
import functools

import numpy as np
import jax
import jax.numpy as jnp
from jax import lax
from jax.experimental import pallas as pl
from jax.experimental.pallas import tpu as pltpu

CHUNK = 64
BAND_CHUNKS = 8
BAND_PAST = BAND_CHUNKS * CHUNK
MAX_REL = 128
PAST_LEN = 1024
EPS = 1e-6
NEG_INF = -1e30

LANES = 128
HEAD_PAIR = LANES
CONV_HALO = 32
Q_TILE = 4 * CHUNK
VMEM_LIMIT = 56 * 1024 * 1024

F32 = jnp.float32
BF16 = jnp.bfloat16


def _rms(x, g):
    ms = jnp.mean(x * x, axis=-1, keepdims=True)
    return x * lax.rsqrt(ms + EPS) * g


def _dot(a, b):
    return jnp.dot(a, b, preferred_element_type=F32)


def _resident(block_shape, index):
    return pl.BlockSpec(block_shape, lambda *_: index, pipeline_mode=pl.Buffered(1))


def _params(*semantics):
    return pltpu.CompilerParams(dimension_semantics=semantics, vmem_limit_bytes=VMEM_LIMIT)


def _ffn_kernel(*refs, d_ff, chunks, has_oproj, has_pe, has_final):
    it = iter(refs)
    x_ref = next(it)
    if has_oproj:
        o_ref, wo_ref = next(it), next(it)
    g_ref, win_ref, wout_ref = next(it), next(it), next(it)
    if has_pe:
        g3_ref, wgate_ref, p_ref, wproj_ref = next(it), next(it), next(it), next(it)
    if has_final:
        gf_ref = next(it)
    out_ref = next(it)

    x = x_ref[...]
    if has_oproj:
        o = jnp.concatenate([o_ref[hp] for hp in range(o_ref.shape[0])], axis=-1)
        x = x + _dot(o, wo_ref[...])
    h = _rms(x, g_ref[...]).astype(BF16)
    acc = None
    for c0, c1 in chunks:
        a = _dot(h, win_ref[:, c0:c1])
        b = _dot(h, win_ref[:, d_ff + c0:d_ff + c1])
        u = (a * jax.nn.sigmoid(a) * b).astype(BF16)
        d = _dot(u, wout_ref[c0:c1, :])
        acc = d if acc is None else acc + d
    x = x + 0.5 * acc
    if has_pe:
        gate = jax.nn.sigmoid(_dot(_rms(x, g3_ref[...]).astype(BF16), wgate_ref[...]))
        x = x + gate * _dot(p_ref[...].astype(BF16), wproj_ref[...])
    if has_final:
        x = _rms(x, gf_ref[...])
    out_ref[...] = x


def _ffn(x, g, w_in, w_out, layer, half, *, tm, oproj=None, pe=None, final_g=None):
    n, d = x.shape
    d_ff = w_out.shape[2]
    step = 512 if d_ff >= 512 else d_ff
    chunks = tuple((c, min(c + step, d_ff)) for c in range(0, d_ff, step))
    row = lambda i: (i, 0)
    args, specs = [x], [pl.BlockSpec((tm, d), row)]
    if oproj is not None:
        o, w_o, li = oproj
        args += [o, w_o]
        specs += [pl.BlockSpec((o.shape[0], tm, HEAD_PAIR), lambda i: (0, i, 0)),
                  _resident((None, d, d), (li, 0, 0))]
    args += [g, w_in, w_out]
    specs += [_resident((1, d), (0, 0)),
              _resident((None, None, d, 2 * d_ff), (layer, half, 0, 0)),
              _resident((None, None, d_ff, d), (layer, half, 0, 0))]
    if pe is not None:
        g3, w_gate, p, w_proj = pe
        pe_dim = p.shape[-1]
        args += [g3, w_gate, p, w_proj]
        specs += [_resident((1, d), (0, 0)),
                  _resident((None, d, d), (layer, 0, 0)),
                  pl.BlockSpec((None, tm, pe_dim), lambda i: (layer, i, 0)),
                  _resident((None, pe_dim, d), (layer, 0, 0))]
    if final_g is not None:
        args += [final_g]
        specs += [_resident((1, d), (0, 0))]
    body = functools.partial(_ffn_kernel, d_ff=d_ff, chunks=chunks, has_oproj=oproj is not None,
                             has_pe=pe is not None, has_final=final_g is not None)
    return pl.pallas_call(
        body, grid=(n // tm,), in_specs=specs, out_specs=pl.BlockSpec((tm, d), row),
        out_shape=jax.ShapeDtypeStruct((n, d), F32), compiler_params=_params("arbitrary"),
        name="ffn")(*args)


def _conv_kernel(x_ref, buf_ref, g1_ref, win_ref, bin_ref, dw_ref, dwb_ref, cg_ref, wout_ref, bout_ref,
                 out_ref, nbuf_ref, ext_ref, *, tm, width, row_block):
    d = x_ref.shape[-1]
    ncb = d // LANES

    @pl.when(pl.program_id(1) == 0)
    def _():
        for cb in range(ncb):
            ext_ref[cb, 0:CONV_HALO, :] = buf_ref[0, :, cb * LANES:(cb + 1) * LANES]

    x = x_ref[0]
    h = _rms(x, g1_ref[...]).astype(BF16)
    ag = _dot(h, win_ref[...]) + bin_ref[...]
    glu = ag[:, :d] * jax.nn.sigmoid(ag[:, d:])
    for cb in range(ncb):
        ext_ref[cb, CONV_HALO:CONV_HALO + tm, :] = glu[:, cb * LANES:(cb + 1) * LANES]

    lead = CONV_HALO - (width - 1)
    cols = []
    for cb in range(ncb):
        lanes = slice(cb * LANES, (cb + 1) * LANES)
        blocks = []
        for r0 in range(0, tm, row_block):
            acc = jnp.broadcast_to(dwb_ref[:, lanes], (row_block, LANES))
            for j in range(width):
                acc = acc + dw_ref[j:j + 1, lanes] * ext_ref[cb, r0 + j + lead:r0 + j + lead + row_block, :]
            blocks.append(acc)
        cols.append(jnp.concatenate(blocks, axis=0) if len(blocks) > 1 else blocks[0])
    y = jnp.concatenate(cols, axis=-1)
    y = _rms(y, cg_ref[...])
    y = (y * jax.nn.sigmoid(y)).astype(BF16)
    out_ref[0] = x + (_dot(y, wout_ref[...]) + bout_ref[...])

    for cb in range(ncb):
        tail = ext_ref[cb, tm:tm + CONV_HALO, :]
        nbuf_ref[0, :, cb * LANES:(cb + 1) * LANES] = tail
        ext_ref[cb, 0:CONV_HALO, :] = tail


def _conv(x, buf, g1, w_in, b_in, dw, dw_b, cg, w_out, b_out, j, *, tm):
    b, s, d = x.shape
    width = dw.shape[1]
    row_block = min(tm, 128)
    tile = lambda bi, ti: (bi, ti, 0)
    specs = [pl.BlockSpec((1, tm, d), tile),
             pl.BlockSpec((1, CONV_HALO, d), lambda bi, ti: (bi, 0, 0)),
             _resident((1, d), (0, 0)),
             _resident((None, d, 2 * d), (j, 0, 0)),
             _resident((None, 1, 2 * d), (j, 0, 0)),
             _resident((None, width, d), (j, 0, 0)),
             _resident((None, 1, d), (j, 0, 0)),
             _resident((None, 1, d), (j, 0, 0)),
             _resident((None, d, d), (j, 0, 0)),
             _resident((None, 1, d), (j, 0, 0))]
    body = functools.partial(_conv_kernel, tm=tm, width=width, row_block=row_block)
    return pl.pallas_call(
        body, grid=(b, s // tm), in_specs=specs,
        out_specs=[pl.BlockSpec((1, tm, d), tile), pl.BlockSpec((1, CONV_HALO, d), lambda bi, ti: (bi, 0, 0))],
        out_shape=[jax.ShapeDtypeStruct((b, s, d), F32), jax.ShapeDtypeStruct((b, CONV_HALO, d), F32)],
        scratch_shapes=[pltpu.VMEM((d // LANES, CONV_HALO + tm, LANES), F32)],
        compiler_params=_params("arbitrary", "arbitrary"), name="conv_module",
    )(x, buf, g1, w_in, b_in, dw, dw_b, cg, w_out, b_out)


def _qkv_kernel(x_ref, g_ref, w_ref, q_ref, k_ref, v_ref, k32_ref, v32_ref, *, scale):
    d = x_ref.shape[-1]
    h = _rms(x_ref[...], g_ref[...]).astype(BF16)
    qkv = _dot(h, w_ref[...])
    for hp in range(d // HEAD_PAIR):
        lanes = slice(hp * HEAD_PAIR, (hp + 1) * HEAD_PAIR)
        q_ref[hp] = (qkv[:, lanes] * scale).astype(BF16)
        k_ref[hp] = qkv[:, d:2 * d][:, lanes].astype(BF16)
        v_ref[hp] = qkv[:, 2 * d:][:, lanes].astype(BF16)
    k32_ref[...] = qkv[:, d:2 * d]
    v32_ref[...] = qkv[:, 2 * d:]


def _qkv(x, g, w_qkv, j, *, tm, tiles_per_tail, scale):
    n, d = x.shape
    nt = n // tm
    hp = d // HEAD_PAIR
    pair_spec = pl.BlockSpec((hp, tm, HEAD_PAIR), lambda i: (0, i, 0))
    tail_spec = pl.BlockSpec((tm, d), lambda i: (i // tiles_per_tail, 0))
    pair_shape = jax.ShapeDtypeStruct((hp, n, HEAD_PAIR), BF16)
    tail_shape = jax.ShapeDtypeStruct((nt // tiles_per_tail * tm, d), F32)
    return pl.pallas_call(
        functools.partial(_qkv_kernel, scale=scale), grid=(nt,),
        in_specs=[pl.BlockSpec((tm, d), lambda i: (i, 0)), _resident((1, d), (0, 0)),
                  _resident((None, d, 3 * d), (j, 0, 0))],
        out_specs=[pair_spec, pair_spec, pair_spec, tail_spec, tail_spec],
        out_shape=[pair_shape, pair_shape, pair_shape, tail_shape, tail_shape],
        compiler_params=_params("arbitrary"), name="qkv_proj")(x, g, w_qkv)


def _own_lanes(hh, head_dim):
    lane = lax.broadcasted_iota(jnp.int32, (1, HEAD_PAIR), 1)
    return (lane < head_dim) if hh == 0 else (lane >= head_dim)


def _scores(q2, keys, biases, hh, head_dim):
    qh = jnp.where(_own_lanes(hh, head_dim), q2, jnp.zeros_like(q2))
    return [lax.dot_general(qh, k, (((1,), (1,)), ((), ())), preferred_element_type=F32) + bias
            for k, bias in zip(keys, biases)]


def _weighted_values(s, values, hh, head_dim):
    mine = _own_lanes(hh, head_dim)
    if len({sj.shape[1] for sj in s}) == 1:
        m = jnp.max(functools.reduce(jnp.maximum, s), axis=-1, keepdims=True)
    else:
        m = functools.reduce(jnp.maximum, [jnp.max(sj, axis=-1, keepdims=True) for sj in s])
    return functools.reduce(jnp.add, [
        _dot(jnp.exp(sj - m).astype(BF16), jnp.where(mine, v, jnp.ones_like(v))) for sj, v in zip(s, values)])


def _normalize_pair(acc0, acc1, head_dim):
    low = _own_lanes(0, head_dim)
    num = jnp.where(low, acc0, acc1)
    den = pltpu.roll(jnp.where(low, acc1, acc0), head_dim, 1)
    return num / den


def _attend_heads(n_heads, q_of, keys_of, values_of, biases_of, store, head_dim):
    s_next = _scores(q_of(0), keys_of(0), biases_of(0), 0, head_dim)
    acc = [None, None]
    for h in range(n_heads):
        s_cur = s_next
        if h + 1 < n_heads:
            s_next = _scores(q_of((h + 1) // 2), keys_of((h + 1) // 2), biases_of(h + 1), (h + 1) % 2, head_dim)
        acc[h % 2] = _weighted_values(s_cur, values_of(h // 2), h % 2, head_dim)
        if h % 2 == 1:
            store(h // 2, _normalize_pair(acc[0], acc[1], head_dim))


def _attn_prompt_kernel(q_ref, k0_ref, k1_ref, k2_ref, v0_ref, v1_ref, v2_ref, bm_ref, o_ref, *, head_dim):
    t = pl.program_id(1)
    k_refs, v_refs = (k0_ref, k1_ref, k2_ref), (v0_ref, v1_ref, v2_ref)

    def store(hp, o):
        o_ref[hp] = o.astype(BF16)

    def run(blocks):
        _attend_heads(2 * q_ref.shape[0], lambda hp: q_ref[hp], lambda hp: [k_refs[j][hp] for j in blocks],
                      lambda hp: [v_refs[j][hp] for j in blocks], lambda h: [bm_ref[h, j] for j in blocks],
                      store, head_dim)

    pl.when(t == 0)(lambda: run((2,)))
    pl.when(t == 1)(lambda: run((1, 2)))
    pl.when(t >= 2)(lambda: run((0, 1, 2)))


def _prompt_bias(table):
    rows, cols = Q_TILE, 3 * Q_TILE
    period = cols + rows - 1
    u = np.concatenate([np.arange(cols), np.arange(-(rows - 1), 0)])
    slots = np.clip(2 * Q_TILE - u, -MAX_REL, MAX_REL) + MAX_REL
    h = table.shape[0]
    line = jnp.tile(table[:, slots], (1, rows))[:, :rows * (period - 1)]
    toeplitz = line.reshape(h, rows, period - 1)[:, :, :cols]
    r = np.arange(rows)[:, None]
    kk = np.arange(cols)[None, :] - 2 * Q_TILE
    qc, kc = r // CHUNK, kk // CHUNK
    mask = (kc <= qc) & (kc >= qc - BAND_CHUNKS)
    bias = jnp.where(mask[None], toeplitz, NEG_INF)
    return bias.reshape(h, Q_TILE, 3, Q_TILE).transpose(0, 2, 1, 3)


def _attn_prompt(q, k, v, table, b, s):
    hp, n, _ = q.shape
    nq = s // Q_TILE
    head_dim = HEAD_PAIR // 2
    bm = _prompt_bias(table)

    def key_spec(j):
        return pl.BlockSpec((hp, Q_TILE, HEAD_PAIR), lambda bi, ti: (0, bi * nq + jnp.maximum(ti - 2 + j, 0), 0))

    tile_spec = pl.BlockSpec((hp, Q_TILE, HEAD_PAIR), lambda bi, ti: (0, bi * nq + ti, 0))
    return pl.pallas_call(
        functools.partial(_attn_prompt_kernel, head_dim=head_dim), grid=(b, nq),
        in_specs=[tile_spec, key_spec(0), key_spec(1), key_spec(2), key_spec(0), key_spec(1), key_spec(2),
                  _resident(bm.shape, (0, 0, 0, 0))],
        out_specs=tile_spec, out_shape=jax.ShapeDtypeStruct((hp, n, HEAD_PAIR), BF16),
        compiler_params=_params("arbitrary", "arbitrary"), name="attn_prompt")(q, k, k, k, v, v, v, bm)


def _attn_sample_kernel(q_ref, kn_ref, vn_ref, ck_ref, cv_ref, bc_ref, bn_ref, o_ref, *, head_dim):
    def lanes(hp):
        return slice(hp * HEAD_PAIR, (hp + 1) * HEAD_PAIR)

    def store(hp, o):
        o_ref[hp] = o.astype(BF16)

    _attend_heads(2 * q_ref.shape[0], lambda hp: q_ref[hp],
                  lambda hp: [ck_ref[0, :, lanes(hp)].astype(BF16), kn_ref[hp]],
                  lambda hp: [cv_ref[0, :, lanes(hp)].astype(BF16), vn_ref[hp]],
                  lambda h: [bc_ref[h], bn_ref[h]], store, head_dim)


def _attn_sample(q, k, v, ck, cv, table, b, t):
    hp, n, _ = q.shape
    w, d = ck.shape[1], ck.shape[2]
    head_dim = HEAD_PAIR // 2
    qpos = PAST_LEN + np.arange(t)
    kpos = np.concatenate([PAST_LEN - w + np.arange(w), qpos])
    rel = np.clip(qpos[:, None] - kpos[None, :], -MAX_REL, MAX_REL) + MAX_REL
    qc, kc = qpos[:, None] // CHUNK, kpos[None, :] // CHUNK
    mask = (kpos[None, :] >= 0) & (kc <= qc) & (kc >= qc - BAND_CHUNKS)
    bias = jnp.where(mask[None], table[:, rel], NEG_INF)
    new_spec = pl.BlockSpec((hp, t, HEAD_PAIR), lambda bi: (0, bi, 0))
    cache_spec = pl.BlockSpec((1, w, d), lambda bi: (bi, 0, 0))
    return pl.pallas_call(
        functools.partial(_attn_sample_kernel, head_dim=head_dim), grid=(b,),
        in_specs=[new_spec, new_spec, new_spec, cache_spec, cache_spec,
                  _resident((bias.shape[0], t, w), (0, 0, 0)), _resident((bias.shape[0], t, t), (0, 0, 0))],
        out_specs=new_spec, out_shape=jax.ShapeDtypeStruct((hp, n, HEAD_PAIR), BF16),
        compiler_params=_params("arbitrary"), name="attn_sample",
    )(q, k, v, ck, cv, bias[:, :, :w], bias[:, :, w:])


def _trunk(x, p, conv_state, cache_k, cache_v, w, *, tm):
    b, s, d = x.shape
    depth = w["norm_g"].shape[0]
    n = b * s
    is_prompt = conv_state is None
    n_heads, head_dim = cache_k.shape[3], cache_k.shape[4]
    width = w["conv_dw"].shape[1]
    p = p.reshape(depth, n, p.shape[-1])
    g = lambda i, k: w["norm_g"][i, k].reshape(1, d)
    x = x.reshape(n, d)
    new_conv, new_k, new_v = [], [], []
    oproj = None
    for i in range(depth):
        j = i // 2
        x = _ffn(x, g(i, 0), w["ffn_w_in"], w["ffn_w_out"], i, 0, tm=tm)
        if i % 2 == 0:
            if is_prompt:
                buf = jnp.zeros((b, CONV_HALO, d), F32)
            else:
                buf = jnp.pad(conv_state[j], ((0, 0), (CONV_HALO - (width - 1), 0), (0, 0)))
            x, nbuf = _conv(x.reshape(b, s, d), buf, g(i, 1), w["conv_w_in"], w["conv_b_in"], w["conv_dw"],
                            w["conv_dw_b"], w["conv_norm_g"], w["conv_w_out"], w["conv_b_out"], j,
                            tm=min(tm, s))
            x = x.reshape(n, d)
            new_conv.append(nbuf[:, CONV_HALO - (width - 1):])
            oproj = None
        else:
            keep = min(BAND_PAST, s) if is_prompt else s
            q, k, v, k32, v32 = _qkv(x, g(i, 1), w["attn_w_qkv"], j, tm=tm,
                                     tiles_per_tail=(s // keep) if is_prompt else 1, scale=head_dim ** -0.5)
            if is_prompt:
                o = _attn_prompt(q, k, v, w["attn_rel_table"][j], b, s)
            else:
                win = cache_k.shape[2]
                o = _attn_sample(q, k, v, cache_k[j].reshape(b, win, d), cache_v[j].reshape(b, win, d),
                                 w["attn_rel_table"][j], b, s)
            new_k.append(k32.reshape(b, keep, n_heads, head_dim))
            new_v.append(v32.reshape(b, keep, n_heads, head_dim))
            oproj = (o, w["attn_w_o"], j)
        x = _ffn(x, g(i, 2), w["ffn_w_in"], w["ffn_w_out"], i, 1, tm=tm, oproj=oproj,
                 pe=(g(i, 3), w["pe_w_gate"], p, w["pe_w_proj"]),
                 final_g=w["final_g"].reshape(1, d) if i == depth - 1 else None)
    return x.reshape(b, s, d), jnp.stack(new_conv), jnp.stack(new_k), jnp.stack(new_v)


def kernel(x_prompt, x_sample, cache_conv, cache_k, cache_v, p_prompt, p_sample, norm_g, final_g, ffn_w_in,
           ffn_w_out, conv_w_in, conv_b_in, conv_dw, conv_dw_b, conv_norm_g, conv_w_out, conv_b_out,
           attn_w_qkv, attn_w_o, attn_rel_table, pe_w_proj, pe_w_gate):
    n_conv = conv_w_in.shape[0]
    w = dict(norm_g=norm_g, final_g=final_g,
             ffn_w_in=ffn_w_in.astype(BF16), ffn_w_out=ffn_w_out.astype(BF16),
             conv_w_in=conv_w_in.astype(BF16), conv_b_in=conv_b_in.reshape(n_conv, 1, -1), conv_dw=conv_dw,
             conv_dw_b=conv_dw_b.reshape(n_conv, 1, -1), conv_norm_g=conv_norm_g.reshape(n_conv, 1, -1),
             conv_w_out=conv_w_out.astype(BF16), conv_b_out=conv_b_out.reshape(n_conv, 1, -1),
             attn_w_qkv=attn_w_qkv.astype(BF16), attn_w_o=attn_w_o.astype(BF16), attn_rel_table=attn_rel_table,
             pe_w_proj=pe_w_proj.astype(BF16), pe_w_gate=pe_w_gate.astype(BF16))
    n_sample = x_sample.shape[0] * x_sample.shape[1]
    y_p, conv_p, k_p, v_p = _trunk(x_prompt, p_prompt, None, cache_k, cache_v, w, tm=512)
    y_s, conv_s, k_s, v_s = _trunk(x_sample, p_sample, cache_conv, cache_k, cache_v, w, tm=min(512, n_sample))
    return (y_p, y_s, conv_p, k_p, v_p, conv_s, k_s, v_s)
```

```python
import functools

import numpy as np
import jax
import jax.numpy as jnp
from jax import lax
from jax.experimental import pallas as pl
from jax.experimental.pallas import tpu as pltpu

CHUNK = 64
BAND_CHUNKS = 8
BAND_PAST = BAND_CHUNKS * CHUNK
MAX_REL = 128
PAST_LEN = 1024
EPS = 1e-6
NEG_INF = -1e30

LANES = 128
HEAD_PAIR = LANES
CONV_HALO = 32
Q_TILE = 4 * CHUNK
VMEM_LIMIT = 56 * 1024 * 1024

F32 = jnp.float32
BF16 = jnp.bfloat16


def _rms(x, g):
    ms = jnp.mean(x * x, axis=-1, keepdims=True)
    return x * lax.rsqrt(ms + EPS) * g


def _dot(a, b):
    return jnp.dot(a, b, preferred_element_type=F32)


def _resident(block_shape, index):
    return pl.BlockSpec(block_shape, lambda *_: index, pipeline_mode=pl.Buffered(1))


def _params(*semantics):
    return pltpu.CompilerParams(dimension_semantics=semantics, vmem_limit_bytes=VMEM_LIMIT)


def _ffn_kernel(*refs, d_ff, chunks, has_oproj, has_pe, has_final):
    it = iter(refs)
    x_ref = next(it)
    if has_oproj:
        o_ref, wo_ref = next(it), next(it)
    g_ref, win_ref, wout_ref = next(it), next(it), next(it)
    if has_pe:
        g3_ref, wgate_ref, p_ref, wproj_ref = next(it), next(it), next(it), next(it)
    if has_final:
        gf_ref = next(it)
    out_ref = next(it)

    x = x_ref[...]
    if has_oproj:
        o = jnp.concatenate([o_ref[hp] for hp in range(o_ref.shape[0])], axis=-1)
        x = x + _dot(o, wo_ref[...])
    h = _rms(x, g_ref[...]).astype(BF16)
    acc = None
    for c0, c1 in chunks:
        a = _dot(h, win_ref[:, c0:c1])
        b = _dot(h, win_ref[:, d_ff + c0:d_ff + c1])
        u = (a * jax.nn.sigmoid(a) * b).astype(BF16)
        d = _dot(u, wout_ref[c0:c1, :])
        acc = d if acc is None else acc + d
    x = x + 0.5 * acc
    if has_pe:
        gate = jax.nn.sigmoid(_dot(_rms(x, g3_ref[...]).astype(BF16), wgate_ref[...]))
        x = x + gate * _dot(p_ref[...].astype(BF16), wproj_ref[...])
    if has_final:
        x = _rms(x, gf_ref[...])
    out_ref[...] = x


def _ffn(x, g, w_in, w_out, layer, half, *, tm, oproj=None, pe=None, final_g=None):
    n, d = x.shape
    d_ff = w_out.shape[2]
    step = 512 if d_ff >= 512 else d_ff
    chunks = tuple((c, min(c + step, d_ff)) for c in range(0, d_ff, step))
    row = lambda i: (i, 0)
    args, specs = [x], [pl.BlockSpec((tm, d), row)]
    if oproj is not None:
        o, w_o, li = oproj
        args += [o, w_o]
        specs += [pl.BlockSpec((o.shape[0], tm, HEAD_PAIR), lambda i: (0, i, 0)),
                  _resident((None, d, d), (li, 0, 0))]
    args += [g, w_in, w_out]
    specs += [_resident((1, d), (0, 0)),
              _resident((None, None, d, 2 * d_ff), (layer, half, 0, 0)),
              _resident((None, None, d_ff, d), (layer, half, 0, 0))]
    if pe is not None:
        g3, w_gate, p, w_proj = pe
        pe_dim = p.shape[-1]
        args += [g3, w_gate, p, w_proj]
        specs += [_resident((1, d), (0, 0)),
                  _resident((None, d, d), (layer, 0, 0)),
                  pl.BlockSpec((None, tm, pe_dim), lambda i: (layer, i, 0)),
                  _resident((None, pe_dim, d), (layer, 0, 0))]
    if final_g is not None:
        args += [final_g]
        specs += [_resident((1, d), (0, 0))]
    body = functools.partial(_ffn_kernel, d_ff=d_ff, chunks=chunks, has_oproj=oproj is not None,
                             has_pe=pe is not None, has_final=final_g is not None)
    return pl.pallas_call(
        body, grid=(n // tm,), in_specs=specs, out_specs=pl.BlockSpec((tm, d), row),
        out_shape=jax.ShapeDtypeStruct((n, d), F32), compiler_params=_params("arbitrary"),
        name="ffn")(*args)


def _conv_kernel(x_ref, buf_ref, g1_ref, win_ref, bin_ref, dw_ref, dwb_ref, cg_ref, wout_ref, bout_ref,
                 out_ref, nbuf_ref, ext_ref, *, tm, width, row_block):
    d = x_ref.shape[-1]
    ncb = d // LANES

    @pl.when(pl.program_id(1) == 0)
    def _():
        for cb in range(ncb):
            ext_ref[cb, 0:CONV_HALO, :] = buf_ref[0, :, cb * LANES:(cb + 1) * LANES]

    x = x_ref[0]
    h = _rms(x, g1_ref[...]).astype(BF16)
    lead = CONV_HALO - (width - 1)
    group = 2 * LANES
    cols = []

    def project(c0):
        cs = slice(c0, c0 + group)
        a = _dot(h, win_ref[:, cs]) + bin_ref[:, cs]
        gate = _dot(h, win_ref[:, d + c0:d + c0 + group]) + bin_ref[:, d + c0:d + c0 + group]
        return a * jax.nn.sigmoid(gate)

    glu_next = project(0)
    for c0 in range(0, d, group):
        glu = glu_next
        if c0 + group < d:
            glu_next = project(c0 + group)
        for cb in range(c0 // LANES, (c0 + group) // LANES):
            lanes = slice(cb * LANES, (cb + 1) * LANES)
            ext_ref[cb, CONV_HALO:CONV_HALO + tm, :] = glu[:, cb * LANES - c0:(cb + 1) * LANES - c0]
            blocks = []
            for r0 in range(0, tm, row_block):
                acc = jnp.broadcast_to(dwb_ref[:, lanes], (row_block, LANES))
                for j in range(width):
                    acc = acc + dw_ref[j:j + 1, lanes] * ext_ref[cb, r0 + j + lead:r0 + j + lead + row_block, :]
                blocks.append(acc)
            cols.append(jnp.concatenate(blocks, axis=0) if len(blocks) > 1 else blocks[0])
    y = jnp.concatenate(cols, axis=-1)
    y = _rms(y, cg_ref[...])
    y = (y * jax.nn.sigmoid(y)).astype(BF16)
    out_ref[0] = x + (_dot(y, wout_ref[...]) + bout_ref[...])

    for cb in range(ncb):
        tail = ext_ref[cb, tm:tm + CONV_HALO, :]
        nbuf_ref[0, :, cb * LANES:(cb + 1) * LANES] = tail
        ext_ref[cb, 0:CONV_HALO, :] = tail


def _conv(x, buf, g1, w_in, b_in, dw, dw_b, cg, w_out, b_out, j, *, tm):
    b, s, d = x.shape
    width = dw.shape[1]
    row_block = min(tm, 128)
    tile = lambda bi, ti: (bi, ti, 0)
    specs = [pl.BlockSpec((1, tm, d), tile),
             pl.BlockSpec((1, CONV_HALO, d), lambda bi, ti: (bi, 0, 0)),
             _resident((1, d), (0, 0)),
             _resident((None, d, 2 * d), (j, 0, 0)),
             _resident((None, 1, 2 * d), (j, 0, 0)),
             _resident((None, width, d), (j, 0, 0)),
             _resident((None, 1, d), (j, 0, 0)),
             _resident((None, 1, d), (j, 0, 0)),
             _resident((None, d, d), (j, 0, 0)),
             _resident((None, 1, d), (j, 0, 0))]
    body = functools.partial(_conv_kernel, tm=tm, width=width, row_block=row_block)
    return pl.pallas_call(
        body, grid=(b, s // tm), in_specs=specs,
        out_specs=[pl.BlockSpec((1, tm, d), tile), pl.BlockSpec((1, CONV_HALO, d), lambda bi, ti: (bi, 0, 0))],
        out_shape=[jax.ShapeDtypeStruct((b, s, d), F32), jax.ShapeDtypeStruct((b, CONV_HALO, d), F32)],
        scratch_shapes=[pltpu.VMEM((d // LANES, CONV_HALO + tm, LANES), F32)],
        compiler_params=_params("arbitrary", "arbitrary"), name="conv_module",
    )(x, buf, g1, w_in, b_in, dw, dw_b, cg, w_out, b_out)


def _qkv_kernel(x_ref, g_ref, w_ref, q_ref, k_ref, v_ref, k32_ref, v32_ref, *, scale):
    d = x_ref.shape[-1]
    h = _rms(x_ref[...], g_ref[...]).astype(BF16)
    qkv = _dot(h, w_ref[...])
    for hp in range(d // HEAD_PAIR):
        lanes = slice(hp * HEAD_PAIR, (hp + 1) * HEAD_PAIR)
        q_ref[hp] = (qkv[:, lanes] * scale).astype(BF16)
        k_ref[hp] = qkv[:, d:2 * d][:, lanes].astype(BF16)
        v_ref[hp] = qkv[:, 2 * d:][:, lanes].astype(BF16)
    k32_ref[...] = qkv[:, d:2 * d]
    v32_ref[...] = qkv[:, 2 * d:]


def _qkv(x, g, w_qkv, j, *, tm, tiles_per_tail, scale):
    n, d = x.shape
    nt = n // tm
    hp = d // HEAD_PAIR
    pair_spec = pl.BlockSpec((hp, tm, HEAD_PAIR), lambda i: (0, i, 0))
    tail_spec = pl.BlockSpec((tm, d), lambda i: (i // tiles_per_tail, 0))
    pair_shape = jax.ShapeDtypeStruct((hp, n, HEAD_PAIR), BF16)
    tail_shape = jax.ShapeDtypeStruct((nt // tiles_per_tail * tm, d), F32)
    return pl.pallas_call(
        functools.partial(_qkv_kernel, scale=scale), grid=(nt,),
        in_specs=[pl.BlockSpec((tm, d), lambda i: (i, 0)), _resident((1, d), (0, 0)),
                  _resident((None, d, 3 * d), (j, 0, 0))],
        out_specs=[pair_spec, pair_spec, pair_spec, tail_spec, tail_spec],
        out_shape=[pair_shape, pair_shape, pair_shape, tail_shape, tail_shape],
        compiler_params=_params("arbitrary"), name="qkv_proj")(x, g, w_qkv)


def _own_lanes(hh, head_dim):
    lane = lax.broadcasted_iota(jnp.int32, (1, HEAD_PAIR), 1)
    return (lane < head_dim) if hh == 0 else (lane >= head_dim)


def _scores(q2, keys, biases, hh, head_dim):
    qh = jnp.where(_own_lanes(hh, head_dim), q2, jnp.zeros_like(q2))
    return [lax.dot_general(qh, k, (((1,), (1,)), ((), ())), preferred_element_type=F32) + bias
            for k, bias in zip(keys, biases)]


def _weighted_values(s, values, hh, head_dim):
    mine = _own_lanes(hh, head_dim)
    if len({sj.shape[1] for sj in s}) == 1:
        m = jnp.max(functools.reduce(jnp.maximum, s), axis=-1, keepdims=True)
    else:
        m = functools.reduce(jnp.maximum, [jnp.max(sj, axis=-1, keepdims=True) for sj in s])
    return functools.reduce(jnp.add, [
        _dot(jnp.exp(sj - m).astype(BF16), jnp.where(mine, v, jnp.ones_like(v))) for sj, v in zip(s, values)])


def _normalize_pair(acc0, acc1, head_dim):
    low = _own_lanes(0, head_dim)
    num = jnp.where(low, acc0, acc1)
    den = pltpu.roll(jnp.where(low, acc1, acc0), head_dim, 1)
    return num / den


def _attend_heads(n_heads, q_of, keys_of, values_of, biases_of, store, head_dim):
    s_next = _scores(q_of(0), keys_of(0), biases_of(0), 0, head_dim)
    acc = [None, None]
    for h in range(n_heads):
        s_cur = s_next
        if h + 1 < n_heads:
            s_next = _scores(q_of((h + 1) // 2), keys_of((h + 1) // 2), biases_of(h + 1), (h + 1) % 2, head_dim)
        acc[h % 2] = _weighted_values(s_cur, values_of(h // 2), h % 2, head_dim)
        if h % 2 == 1:
            store(h // 2, _normalize_pair(acc[0], acc[1], head_dim))


def _attn_prompt_kernel(q_ref, k0_ref, k1_ref, k2_ref, v0_ref, v1_ref, v2_ref, line_ref, o_ref, bm_ref, *,
                        head_dim):
    t = pl.program_id(1)
    k_refs, v_refs = (k0_ref, k1_ref, k2_ref), (v0_ref, v1_ref, v2_ref)

    @pl.when((pl.program_id(0) == 0) & (t == 0))
    def _():
        _fill_prompt_bias(line_ref, bm_ref)

    def store(hp, o):
        o_ref[hp] = o.astype(BF16)

    def run(blocks):
        _attend_heads(2 * q_ref.shape[0], lambda hp: q_ref[hp], lambda hp: [k_refs[j][hp] for j in blocks],
                      lambda hp: [v_refs[j][hp] for j in blocks], lambda h: [bm_ref[h, j] for j in blocks],
                      store, head_dim)

    pl.when(t == 0)(lambda: run((2,)))
    pl.when(t == 1)(lambda: run((1, 2)))
    pl.when(t >= 2)(lambda: run((0, 1, 2)))


def _bias_line(table):
    rows, cols = Q_TILE, 3 * Q_TILE
    u = np.arange(rows + cols)
    u = np.where(u < cols, u, u - (rows + cols))
    slots = np.clip(2 * Q_TILE - u, -MAX_REL, MAX_REL) + MAX_REL
    return table[:, slots][:, None, :]


def _fill_prompt_bias(line_ref, bm_ref):
    n_heads, rows = bm_ref.shape[0], bm_ref.shape[2]
    width = line_ref.shape[-1]
    cols = bm_ref.shape[1] * rows
    qc = lax.broadcasted_iota(jnp.int32, (rows, cols), 0) // CHUNK
    kc = lax.broadcasted_iota(jnp.int32, (rows, cols), 1) // CHUNK - (cols - rows) // CHUNK
    band = (kc <= qc) & (kc >= qc - BAND_CHUNKS)
    for h in range(n_heads):
        rotated = pltpu.roll(jnp.broadcast_to(line_ref[h], (rows, width)), 0, 1, stride=1, stride_axis=0)
        bias = jnp.where(band, rotated[:, :cols], NEG_INF)
        for j in range(bm_ref.shape[1]):
            bm_ref[h, j] = bias[:, j * rows:(j + 1) * rows]


def _attn_prompt(q, k, v, table, b, s):
    hp, n, _ = q.shape
    nq = s // Q_TILE
    head_dim = HEAD_PAIR // 2
    line = _bias_line(table)
    n_heads = table.shape[0]

    def key_spec(j):
        return pl.BlockSpec((hp, Q_TILE, HEAD_PAIR), lambda bi, ti: (0, bi * nq + jnp.maximum(ti - 2 + j, 0), 0))

    tile_spec = pl.BlockSpec((hp, Q_TILE, HEAD_PAIR), lambda bi, ti: (0, bi * nq + ti, 0))
    return pl.pallas_call(
        functools.partial(_attn_prompt_kernel, head_dim=head_dim), grid=(b, nq),
        in_specs=[tile_spec, key_spec(0), key_spec(1), key_spec(2), key_spec(0), key_spec(1), key_spec(2),
                  _resident(line.shape, (0, 0, 0))],
        out_specs=tile_spec, out_shape=jax.ShapeDtypeStruct((hp, n, HEAD_PAIR), BF16),
        scratch_shapes=[pltpu.VMEM((n_heads, 3, Q_TILE, Q_TILE), F32)],
        compiler_params=_params("arbitrary", "arbitrary"), name="attn_prompt")(q, k, k, k, v, v, v, line)


def _attn_sample_kernel(q_ref, kn_ref, vn_ref, ck_ref, cv_ref, bc_ref, bn_ref, o_ref, *, head_dim):
    def lanes(hp):
        return slice(hp * HEAD_PAIR, (hp + 1) * HEAD_PAIR)

    def store(hp, o):
        o_ref[hp] = o.astype(BF16)

    _attend_heads(2 * q_ref.shape[0], lambda hp: q_ref[hp],
                  lambda hp: [ck_ref[0, :, lanes(hp)].astype(BF16), kn_ref[hp]],
                  lambda hp: [cv_ref[0, :, lanes(hp)].astype(BF16), vn_ref[hp]],
                  lambda h: [bc_ref[h], bn_ref[h]], store, head_dim)


def _attn_sample(q, k, v, ck, cv, table, b, t):
    hp, n, _ = q.shape
    w, d = ck.shape[1], ck.shape[2]
    head_dim = HEAD_PAIR // 2
    qpos = PAST_LEN + np.arange(t)
    kpos = np.concatenate([PAST_LEN - w + np.arange(w), qpos])
    rel = np.clip(qpos[:, None] - kpos[None, :], -MAX_REL, MAX_REL) + MAX_REL
    qc, kc = qpos[:, None] // CHUNK, kpos[None, :] // CHUNK
    mask = (kpos[None, :] >= 0) & (kc <= qc) & (kc >= qc - BAND_CHUNKS)
    bias = jnp.where(mask[None], table[:, rel], NEG_INF)
    new_spec = pl.BlockSpec((hp, t, HEAD_PAIR), lambda bi: (0, bi, 0))
    cache_spec = pl.BlockSpec((1, w, d), lambda bi: (bi, 0, 0))
    return pl.pallas_call(
        functools.partial(_attn_sample_kernel, head_dim=head_dim), grid=(b,),
        in_specs=[new_spec, new_spec, new_spec, cache_spec, cache_spec,
                  _resident((bias.shape[0], t, w), (0, 0, 0)), _resident((bias.shape[0], t, t), (0, 0, 0))],
        out_specs=new_spec, out_shape=jax.ShapeDtypeStruct((hp, n, HEAD_PAIR), BF16),
        compiler_params=_params("arbitrary"), name="attn_sample",
    )(q, k, v, ck, cv, bias[:, :, :w], bias[:, :, w:])


def _trunk(x, p, conv_state, cache_k, cache_v, w, *, tm):
    b, s, d = x.shape
    depth = w["norm_g"].shape[0]
    n = b * s
    is_prompt = conv_state is None
    n_heads, head_dim = cache_k.shape[3], cache_k.shape[4]
    width = w["conv_dw"].shape[1]
    p = p.reshape(depth, n, p.shape[-1])
    g = lambda i, k: w["norm_g"][i, k].reshape(1, d)
    x = x.reshape(n, d)
    new_conv, new_k, new_v = [], [], []
    oproj = None
    for i in range(depth):
        j = i // 2
        x = _ffn(x, g(i, 0), w["ffn_w_in"], w["ffn_w_out"], i, 0, tm=tm)
        if i % 2 == 0:
            if is_prompt:
                buf = jnp.zeros((b, CONV_HALO, d), F32)
            else:
                buf = jnp.pad(conv_state[j], ((0, 0), (CONV_HALO - (width - 1), 0), (0, 0)))
            x, nbuf = _conv(x.reshape(b, s, d), buf, g(i, 1), w["conv_w_in"], w["conv_b_in"], w["conv_dw"],
                            w["conv_dw_b"], w["conv_norm_g"], w["conv_w_out"], w["conv_b_out"], j,
                            tm=min(tm, s))
            x = x.reshape(n, d)
            new_conv.append(nbuf[:, CONV_HALO - (width - 1):])
            oproj = None
        else:
            keep = min(BAND_PAST, s) if is_prompt else s
            q, k, v, k32, v32 = _qkv(x, g(i, 1), w["attn_w_qkv"], j, tm=tm,
                                     tiles_per_tail=(s // keep) if is_prompt else 1, scale=head_dim ** -0.5)
            if is_prompt:
                o = _attn_prompt(q, k, v, w["attn_rel_table"][j], b, s)
            else:
                win = cache_k.shape[2]
                o = _attn_sample(q, k, v, cache_k[j].reshape(b, win, d), cache_v[j].reshape(b, win, d),
                                 w["attn_rel_table"][j], b, s)
            new_k.append(k32.reshape(b, keep, n_heads, head_dim))
            new_v.append(v32.reshape(b, keep, n_heads, head_dim))
            oproj = (o, w["attn_w_o"], j)
        x = _ffn(x, g(i, 2), w["ffn_w_in"], w["ffn_w_out"], i, 1, tm=tm, oproj=oproj,
                 pe=(g(i, 3), w["pe_w_gate"], p, w["pe_w_proj"]),
                 final_g=w["final_g"].reshape(1, d) if i == depth - 1 else None)
    return x.reshape(b, s, d), jnp.stack(new_conv), jnp.stack(new_k), jnp.stack(new_v)


def kernel(x_prompt, x_sample, cache_conv, cache_k, cache_v, p_prompt, p_sample, norm_g, final_g, ffn_w_in,
           ffn_w_out, conv_w_in, conv_b_in, conv_dw, conv_dw_b, conv_norm_g, conv_w_out, conv_b_out,
           attn_w_qkv, attn_w_o, attn_rel_table, pe_w_proj, pe_w_gate):
    n_conv = conv_w_in.shape[0]
    w = dict(norm_g=norm_g, final_g=final_g,
             ffn_w_in=ffn_w_in.astype(BF16), ffn_w_out=ffn_w_out.astype(BF16),
             conv_w_in=conv_w_in.astype(BF16), conv_b_in=conv_b_in.reshape(n_conv, 1, -1), conv_dw=conv_dw,
             conv_dw_b=conv_dw_b.reshape(n_conv, 1, -1), conv_norm_g=conv_norm_g.reshape(n_conv, 1, -1),
             conv_w_out=conv_w_out.astype(BF16), conv_b_out=conv_b_out.reshape(n_conv, 1, -1),
             attn_w_qkv=attn_w_qkv.astype(BF16), attn_w_o=attn_w_o.astype(BF16), attn_rel_table=attn_rel_table,
             pe_w_proj=pe_w_proj.astype(BF16), pe_w_gate=pe_w_gate.astype(BF16))
    n_sample = x_sample.shape[0] * x_sample.shape[1]
    y_p, conv_p, k_p, v_p = _trunk(x_prompt, p_prompt, None, cache_k, cache_v, w, tm=512)
    y_s, conv_s, k_s, v_s = _trunk(x_sample, p_sample, cache_conv, cache_k, cache_v, w, tm=min(512, n_sample))
    return (y_p, y_s, conv_p, k_p, v_p, conv_s, k_s, v_s)
```

```python
import functools
from typing import NamedTuple

import numpy as np
import jax
import jax.numpy as jnp
from jax import lax
from jax.experimental import pallas as pl
from jax.experimental.pallas import tpu as pltpu

CHUNK = 64
BAND_CHUNKS = 8
BAND_PAST = BAND_CHUNKS * CHUNK
MAX_REL = 128
PAST_LEN = 1024
EPS = 1e-6
NEG_INF = -1e30

LANES = 128
HEAD_PAIR = LANES
CONV_HALO = 32
Q_TILE = 4 * CHUNK
VMEM_LIMIT = 56 * 1024 * 1024

F32 = jnp.float32
BF16 = jnp.bfloat16


def _rms(x, g):
    ms = jnp.mean(x * x, axis=-1, keepdims=True)
    return x * lax.rsqrt(ms + EPS) * g


def _dot(a, b):
    return jnp.dot(a, b, preferred_element_type=F32)


def _resident(block_shape, index):
    return pl.BlockSpec(block_shape, lambda *_: index, pipeline_mode=pl.Buffered(1))


def _params(*semantics):
    return pltpu.CompilerParams(dimension_semantics=semantics, vmem_limit_bytes=VMEM_LIMIT)


def _ffn_kernel(*refs, d_ff, chunks, has_oproj, has_pe, has_final):
    it = iter(refs)
    x_ref = next(it)
    if has_oproj:
        o_ref, wo_ref = next(it), next(it)
    g_ref, win_ref, wout_ref = next(it), next(it), next(it)
    if has_pe:
        g3_ref, wgate_ref, p_ref, wproj_ref = next(it), next(it), next(it), next(it)
    if has_final:
        gf_ref = next(it)
    out_ref = next(it)

    x = x_ref[...]
    if has_oproj:
        o = jnp.concatenate([o_ref[hp] for hp in range(o_ref.shape[0])], axis=-1)
        x = x + _dot(o, wo_ref[...])
    h = _rms(x, g_ref[...]).astype(BF16)
    acc = None
    for c0, c1 in chunks:
        a = _dot(h, win_ref[:, c0:c1])
        b = _dot(h, win_ref[:, d_ff + c0:d_ff + c1])
        u = (a * jax.nn.sigmoid(a) * b).astype(BF16)
        d = _dot(u, wout_ref[c0:c1, :])
        acc = d if acc is None else acc + d
    x = x + 0.5 * acc
    if has_pe:
        gate = jax.nn.sigmoid(_dot(_rms(x, g3_ref[...]).astype(BF16), wgate_ref[...]))
        x = x + gate * _dot(p_ref[...].astype(BF16), wproj_ref[...])
    if has_final:
        x = _rms(x, gf_ref[...])
    out_ref[...] = x


def _ffn(x, g, w_in, w_out, layer, half, *, tm, oproj=None, pe=None, final_g=None):
    n, d = x.shape
    d_ff = w_out.shape[2]
    step = 512 if d_ff >= 512 else d_ff
    chunks = tuple((c, min(c + step, d_ff)) for c in range(0, d_ff, step))
    row = lambda i: (i, 0)
    args, specs = [x], [pl.BlockSpec((tm, d), row)]
    if oproj is not None:
        o, w_o, li = oproj
        args += [o, w_o]
        specs += [pl.BlockSpec((o.shape[0], tm, HEAD_PAIR), lambda i: (0, i, 0)),
                  _resident((None, d, d), (li, 0, 0))]
    args += [g, w_in, w_out]
    specs += [_resident((1, d), (0, 0)),
              _resident((None, None, d, 2 * d_ff), (layer, half, 0, 0)),
              _resident((None, None, d_ff, d), (layer, half, 0, 0))]
    if pe is not None:
        g3, w_gate, p, w_proj = pe
        pe_dim = p.shape[-1]
        args += [g3, w_gate, p, w_proj]
        specs += [_resident((1, d), (0, 0)),
                  _resident((None, d, d), (layer, 0, 0)),
                  pl.BlockSpec((None, tm, pe_dim), lambda i: (layer, i, 0)),
                  _resident((None, pe_dim, d), (layer, 0, 0))]
    if final_g is not None:
        args += [final_g]
        specs += [_resident((1, d), (0, 0))]
    body = functools.partial(_ffn_kernel, d_ff=d_ff, chunks=chunks, has_oproj=oproj is not None,
                             has_pe=pe is not None, has_final=final_g is not None)
    return pl.pallas_call(
        body, grid=(n // tm,), in_specs=specs, out_specs=pl.BlockSpec((tm, d), row),
        out_shape=jax.ShapeDtypeStruct((n, d), F32), compiler_params=_params("arbitrary"),
        name="ffn")(*args)


def _conv_kernel(x_ref, buf_ref, g1_ref, win_ref, bin_ref, dw_ref, dwb_ref, cg_ref, wout_ref, bout_ref,
                 out_ref, nbuf_ref, ext_ref, *, tm, width, row_block):
    d = x_ref.shape[-1]
    ncb = d // LANES

    @pl.when(pl.program_id(1) == 0)
    def _():
        for cb in range(ncb):
            ext_ref[cb, 0:CONV_HALO, :] = buf_ref[0, :, cb * LANES:(cb + 1) * LANES]

    x = x_ref[0]
    h = _rms(x, g1_ref[...]).astype(BF16)
    lead = CONV_HALO - (width - 1)
    group = 2 * LANES
    cols = []

    def project(c0):
        cs = slice(c0, c0 + group)
        a = _dot(h, win_ref[:, cs]) + bin_ref[:, cs]
        gate = _dot(h, win_ref[:, d + c0:d + c0 + group]) + bin_ref[:, d + c0:d + c0 + group]
        return a * jax.nn.sigmoid(gate)

    glu_next = project(0)
    for c0 in range(0, d, group):
        glu = glu_next
        if c0 + group < d:
            glu_next = project(c0 + group)
        for cb in range(c0 // LANES, (c0 + group) // LANES):
            lanes = slice(cb * LANES, (cb + 1) * LANES)
            ext_ref[cb, CONV_HALO:CONV_HALO + tm, :] = glu[:, cb * LANES - c0:(cb + 1) * LANES - c0]
            blocks = []
            for r0 in range(0, tm, row_block):
                acc = jnp.broadcast_to(dwb_ref[:, lanes], (row_block, LANES))
                for j in range(width):
                    acc = acc + dw_ref[j:j + 1, lanes] * ext_ref[cb, r0 + j + lead:r0 + j + lead + row_block, :]
                blocks.append(acc)
            cols.append(jnp.concatenate(blocks, axis=0) if len(blocks) > 1 else blocks[0])
    y = jnp.concatenate(cols, axis=-1)
    y = _rms(y, cg_ref[...])
    y = (y * jax.nn.sigmoid(y)).astype(BF16)
    out_ref[0] = x + (_dot(y, wout_ref[...]) + bout_ref[...])

    for cb in range(ncb):
        tail = ext_ref[cb, tm:tm + CONV_HALO, :]
        nbuf_ref[0, :, cb * LANES:(cb + 1) * LANES] = tail
        ext_ref[cb, 0:CONV_HALO, :] = tail


def _conv(x, buf, g1, w_in, b_in, dw, dw_b, cg, w_out, b_out, j, *, tm):
    b, s, d = x.shape
    width = dw.shape[1]
    row_block = min(tm, 128)
    tile = lambda bi, ti: (bi, ti, 0)
    specs = [pl.BlockSpec((1, tm, d), tile),
             pl.BlockSpec((1, CONV_HALO, d), lambda bi, ti: (bi, 0, 0)),
             _resident((1, d), (0, 0)),
             _resident((None, d, 2 * d), (j, 0, 0)),
             _resident((None, 1, 2 * d), (j, 0, 0)),
             _resident((None, width, d), (j, 0, 0)),
             _resident((None, 1, d), (j, 0, 0)),
             _resident((None, 1, d), (j, 0, 0)),
             _resident((None, d, d), (j, 0, 0)),
             _resident((None, 1, d), (j, 0, 0))]
    body = functools.partial(_conv_kernel, tm=tm, width=width, row_block=row_block)
    return pl.pallas_call(
        body, grid=(b, s // tm), in_specs=specs,
        out_specs=[pl.BlockSpec((1, tm, d), tile), pl.BlockSpec((1, CONV_HALO, d), lambda bi, ti: (bi, 0, 0))],
        out_shape=[jax.ShapeDtypeStruct((b, s, d), F32), jax.ShapeDtypeStruct((b, CONV_HALO, d), F32)],
        scratch_shapes=[pltpu.VMEM((d // LANES, CONV_HALO + tm, LANES), F32)],
        compiler_params=_params("arbitrary", "arbitrary"), name="conv_module",
    )(x, buf, g1, w_in, b_in, dw, dw_b, cg, w_out, b_out)


def _qkv_kernel(x_ref, g_ref, w_ref, q_ref, k_ref, v_ref, k32_ref, v32_ref, *, scale):
    d = x_ref.shape[-1]
    h = _rms(x_ref[...], g_ref[...]).astype(BF16)
    qkv = _dot(h, w_ref[...])
    for hp in range(d // HEAD_PAIR):
        lanes = slice(hp * HEAD_PAIR, (hp + 1) * HEAD_PAIR)
        q_ref[hp] = (qkv[:, lanes] * scale).astype(BF16)
        k_ref[hp] = qkv[:, d:2 * d][:, lanes].astype(BF16)
        v_ref[hp] = qkv[:, 2 * d:][:, lanes].astype(BF16)
    k32_ref[...] = qkv[:, d:2 * d]
    v32_ref[...] = qkv[:, 2 * d:]


def _qkv(x, g, w_qkv, j, *, tm, tiles_per_tail, scale):
    n, d = x.shape
    nt = n // tm
    hp = d // HEAD_PAIR
    pair_spec = pl.BlockSpec((hp, tm, HEAD_PAIR), lambda i: (0, i, 0))
    tail_spec = pl.BlockSpec((tm, d), lambda i: (i // tiles_per_tail, 0))
    pair_shape = jax.ShapeDtypeStruct((hp, n, HEAD_PAIR), BF16)
    tail_shape = jax.ShapeDtypeStruct((nt // tiles_per_tail * tm, d), F32)
    return pl.pallas_call(
        functools.partial(_qkv_kernel, scale=scale), grid=(nt,),
        in_specs=[pl.BlockSpec((tm, d), lambda i: (i, 0)), _resident((1, d), (0, 0)),
                  _resident((None, d, 3 * d), (j, 0, 0))],
        out_specs=[pair_spec, pair_spec, pair_spec, tail_spec, tail_spec],
        out_shape=[pair_shape, pair_shape, pair_shape, tail_shape, tail_shape],
        compiler_params=_params("arbitrary"), name="qkv_proj")(x, g, w_qkv)


def _own_lanes(hh, head_dim):
    lane = lax.broadcasted_iota(jnp.int32, (1, HEAD_PAIR), 1)
    return (lane < head_dim) if hh == 0 else (lane >= head_dim)


class _FeatureMajor(NamedTuple):
    block: jax.Array


def _dot_nt(a, b):
    return lax.dot_general(a, b, (((1,), (1,)), ((), ())), preferred_element_type=F32)


def _scores(q2, keys, biases, hh, head_dim):
    qh = jnp.where(_own_lanes(hh, head_dim), q2, jnp.zeros_like(q2))
    return [(_dot(qh, k.block) if isinstance(k, _FeatureMajor) else _dot_nt(qh, k)) + bias
            for k, bias in zip(keys, biases)]


def _weighted_values(s, values, hh, head_dim):
    mine = _own_lanes(hh, head_dim)
    if len({sj.shape[1] for sj in s}) == 1:
        m = jnp.max(functools.reduce(jnp.maximum, s), axis=-1, keepdims=True)
    else:
        m = functools.reduce(jnp.maximum, [jnp.max(sj, axis=-1, keepdims=True) for sj in s])

    def weighted(p, v):
        if isinstance(v, _FeatureMajor):
            row = lax.broadcasted_iota(jnp.int32, (HEAD_PAIR, 1), 0)
            mine_rows = (row < head_dim) if hh == 0 else (row >= head_dim)
            return _dot_nt(p, jnp.where(mine_rows, v.block, jnp.ones_like(v.block)))
        return _dot(p, jnp.where(mine, v, jnp.ones_like(v)))

    return functools.reduce(jnp.add, [weighted(jnp.exp(sj - m).astype(BF16), v) for sj, v in zip(s, values)])


def _normalize_pair(acc0, acc1, head_dim):
    low = _own_lanes(0, head_dim)
    num = jnp.where(low, acc0, acc1)
    den = pltpu.roll(jnp.where(low, acc1, acc0), head_dim, 1)
    return num / den


def _attend_heads(n_heads, q_of, keys_of, values_of, biases_of, store, head_dim):
    s_next = _scores(q_of(0), keys_of(0), biases_of(0), 0, head_dim)
    acc = [None, None]
    for h in range(n_heads):
        s_cur = s_next
        if h + 1 < n_heads:
            s_next = _scores(q_of((h + 1) // 2), keys_of((h + 1) // 2), biases_of(h + 1), (h + 1) % 2, head_dim)
        acc[h % 2] = _weighted_values(s_cur, values_of(h // 2), h % 2, head_dim)
        if h % 2 == 1:
            store(h // 2, _normalize_pair(acc[0], acc[1], head_dim))


def _attn_prompt_kernel(q_ref, k0_ref, k1_ref, k2_ref, v0_ref, v1_ref, v2_ref, line_ref, o_ref, bm_ref, *,
                        head_dim):
    t = pl.program_id(1)
    k_refs, v_refs = (k0_ref, k1_ref, k2_ref), (v0_ref, v1_ref, v2_ref)

    @pl.when((pl.program_id(0) == 0) & (t == 0))
    def _():
        _fill_prompt_bias(line_ref, bm_ref)

    def store(hp, o):
        o_ref[hp] = o.astype(BF16)

    def run(blocks):
        _attend_heads(2 * q_ref.shape[0], lambda hp: q_ref[hp], lambda hp: [k_refs[j][hp] for j in blocks],
                      lambda hp: [v_refs[j][hp] for j in blocks], lambda h: [bm_ref[h, j] for j in blocks],
                      store, head_dim)

    pl.when(t == 0)(lambda: run((2,)))
    pl.when(t == 1)(lambda: run((1, 2)))
    pl.when(t >= 2)(lambda: run((0, 1, 2)))


def _bias_line(table):
    rows, cols = Q_TILE, 3 * Q_TILE
    u = np.arange(rows + cols)
    u = np.where(u < cols, u, u - (rows + cols))
    slots = np.clip(2 * Q_TILE - u, -MAX_REL, MAX_REL) + MAX_REL
    return table[:, slots][:, None, :]


def _fill_prompt_bias(line_ref, bm_ref):
    n_heads, rows = bm_ref.shape[0], bm_ref.shape[2]
    width = line_ref.shape[-1]
    cols = bm_ref.shape[1] * rows
    qc = lax.broadcasted_iota(jnp.int32, (rows, cols), 0) // CHUNK
    kc = lax.broadcasted_iota(jnp.int32, (rows, cols), 1) // CHUNK - (cols - rows) // CHUNK
    band = (kc <= qc) & (kc >= qc - BAND_CHUNKS)
    for h in range(n_heads):
        rotated = pltpu.roll(jnp.broadcast_to(line_ref[h], (rows, width)), 0, 1, stride=1, stride_axis=0)
        bias = jnp.where(band, rotated[:, :cols], NEG_INF)
        for j in range(bm_ref.shape[1]):
            bm_ref[h, j] = bias[:, j * rows:(j + 1) * rows]


def _attn_prompt(q, k, v, table, b, s):
    hp, n, _ = q.shape
    nq = s // Q_TILE
    head_dim = HEAD_PAIR // 2
    line = _bias_line(table)
    n_heads = table.shape[0]

    def key_spec(j):
        return pl.BlockSpec((hp, Q_TILE, HEAD_PAIR), lambda bi, ti: (0, bi * nq + jnp.maximum(ti - 2 + j, 0), 0))

    tile_spec = pl.BlockSpec((hp, Q_TILE, HEAD_PAIR), lambda bi, ti: (0, bi * nq + ti, 0))
    return pl.pallas_call(
        functools.partial(_attn_prompt_kernel, head_dim=head_dim), grid=(b, nq),
        in_specs=[tile_spec, key_spec(0), key_spec(1), key_spec(2), key_spec(0), key_spec(1), key_spec(2),
                  _resident(line.shape, (0, 0, 0))],
        out_specs=tile_spec, out_shape=jax.ShapeDtypeStruct((hp, n, HEAD_PAIR), BF16),
        scratch_shapes=[pltpu.VMEM((n_heads, 3, Q_TILE, Q_TILE), F32)],
        compiler_params=_params("arbitrary", "arbitrary"), name="attn_prompt")(q, k, k, k, v, v, v, line)


def _attn_sample_kernel(q_ref, kn_ref, vn_ref, ck_ref, cv_ref, bc_ref, bn_ref, o_ref, *, head_dim):
    def lanes(hp):
        return slice(hp * HEAD_PAIR, (hp + 1) * HEAD_PAIR)

    def store(hp, o):
        o_ref[hp] = o.astype(BF16)

    _attend_heads(2 * q_ref.shape[0], lambda hp: q_ref[hp],
                  lambda hp: [_FeatureMajor(ck_ref[lanes(hp), :].astype(BF16)), kn_ref[hp]],
                  lambda hp: [_FeatureMajor(cv_ref[lanes(hp), :].astype(BF16)), vn_ref[hp]],
                  lambda h: [bc_ref[h], bn_ref[h]], store, head_dim)


def _attn_sample(q, k, v, ck, cv, j, table, b, t):
    hp, n, _ = q.shape
    d, w = ck.shape[2], ck.shape[3]
    head_dim = HEAD_PAIR // 2
    qpos = PAST_LEN + np.arange(t)
    kpos = np.concatenate([PAST_LEN - w + np.arange(w), qpos])
    rel = np.clip(qpos[:, None] - kpos[None, :], -MAX_REL, MAX_REL) + MAX_REL
    qc, kc = qpos[:, None] // CHUNK, kpos[None, :] // CHUNK
    mask = (kpos[None, :] >= 0) & (kc <= qc) & (kc >= qc - BAND_CHUNKS)
    bias = jnp.where(mask[None], table[:, rel], NEG_INF)
    new_spec = pl.BlockSpec((hp, t, HEAD_PAIR), lambda bi: (0, bi, 0))
    cache_spec = pl.BlockSpec((None, None, d, w), lambda bi: (j, bi, 0, 0))
    return pl.pallas_call(
        functools.partial(_attn_sample_kernel, head_dim=head_dim), grid=(b,),
        in_specs=[new_spec, new_spec, new_spec, cache_spec, cache_spec,
                  _resident((bias.shape[0], t, w), (0, 0, 0)), _resident((bias.shape[0], t, t), (0, 0, 0))],
        out_specs=new_spec, out_shape=jax.ShapeDtypeStruct((hp, n, HEAD_PAIR), BF16),
        compiler_params=_params("arbitrary"), name="attn_sample",
    )(q, k, v, ck, cv, bias[:, :, :w], bias[:, :, w:])


def _trunk(x, p, conv_state, cache_k, cache_v, w, *, tm, n_heads):
    b, s, d = x.shape
    depth = w["norm_g"].shape[0]
    n = b * s
    is_prompt = conv_state is None
    head_dim = d // n_heads
    width = w["conv_dw"].shape[1]
    p = p.reshape(depth, n, p.shape[-1])
    g = lambda i, k: w["norm_g"][i, k].reshape(1, d)
    x = x.reshape(n, d)
    new_conv, new_k, new_v = [], [], []
    oproj = None
    for i in range(depth):
        j = i // 2
        x = _ffn(x, g(i, 0), w["ffn_w_in"], w["ffn_w_out"], i, 0, tm=tm)
        if i % 2 == 0:
            if is_prompt:
                buf = jnp.zeros((b, CONV_HALO, d), F32)
            else:
                buf = jnp.pad(conv_state[j], ((0, 0), (CONV_HALO - (width - 1), 0), (0, 0)))
            x, nbuf = _conv(x.reshape(b, s, d), buf, g(i, 1), w["conv_w_in"], w["conv_b_in"], w["conv_dw"],
                            w["conv_dw_b"], w["conv_norm_g"], w["conv_w_out"], w["conv_b_out"], j,
                            tm=min(tm, s))
            x = x.reshape(n, d)
            new_conv.append(nbuf[:, CONV_HALO - (width - 1):])
            oproj = None
        else:
            keep = min(BAND_PAST, s) if is_prompt else s
            q, k, v, k32, v32 = _qkv(x, g(i, 1), w["attn_w_qkv"], j, tm=tm,
                                     tiles_per_tail=(s // keep) if is_prompt else 1, scale=head_dim ** -0.5)
            if is_prompt:
                o = _attn_prompt(q, k, v, w["attn_rel_table"][j], b, s)
            else:
                o = _attn_sample(q, k, v, cache_k, cache_v, j, w["attn_rel_table"][j], b, s)
            new_k.append(k32.reshape(b, keep, n_heads, head_dim))
            new_v.append(v32.reshape(b, keep, n_heads, head_dim))
            oproj = (o, w["attn_w_o"], j)
        x = _ffn(x, g(i, 2), w["ffn_w_in"], w["ffn_w_out"], i, 1, tm=tm, oproj=oproj,
                 pe=(g(i, 3), w["pe_w_gate"], p, w["pe_w_proj"]),
                 final_g=w["final_g"].reshape(1, d) if i == depth - 1 else None)
    return x.reshape(b, s, d), jnp.stack(new_conv), jnp.stack(new_k), jnp.stack(new_v)


def kernel(x_prompt, x_sample, cache_conv, cache_k, cache_v, p_prompt, p_sample, norm_g, final_g, ffn_w_in,
           ffn_w_out, conv_w_in, conv_b_in, conv_dw, conv_dw_b, conv_norm_g, conv_w_out, conv_b_out,
           attn_w_qkv, attn_w_o, attn_rel_table, pe_w_proj, pe_w_gate):
    n_conv = conv_w_in.shape[0]
    w = dict(norm_g=norm_g, final_g=final_g,
             ffn_w_in=ffn_w_in.astype(BF16), ffn_w_out=ffn_w_out.astype(BF16),
             conv_w_in=conv_w_in.astype(BF16), conv_b_in=conv_b_in.reshape(n_conv, 1, -1), conv_dw=conv_dw,
             conv_dw_b=conv_dw_b.reshape(n_conv, 1, -1), conv_norm_g=conv_norm_g.reshape(n_conv, 1, -1),
             conv_w_out=conv_w_out.astype(BF16), conv_b_out=conv_b_out.reshape(n_conv, 1, -1),
             attn_w_qkv=attn_w_qkv.astype(BF16), attn_w_o=attn_w_o.astype(BF16), attn_rel_table=attn_rel_table,
             pe_w_proj=pe_w_proj.astype(BF16), pe_w_gate=pe_w_gate.astype(BF16))
    n_sample = x_sample.shape[0] * x_sample.shape[1]
    n_attn, n_streams, window, n_heads, head_dim = cache_k.shape
    ck = jnp.transpose(cache_k, (0, 1, 3, 4, 2)).reshape(n_attn, n_streams, n_heads * head_dim, window)
    cv = jnp.transpose(cache_v, (0, 1, 3, 4, 2)).reshape(n_attn, n_streams, n_heads * head_dim, window)
    y_p, conv_p, k_p, v_p = _trunk(x_prompt, p_prompt, None, None, None, w, tm=512, n_heads=n_heads)
    y_s, conv_s, k_s, v_s = _trunk(x_sample, p_sample, cache_conv, ck, cv, w, tm=min(512, n_sample), n_heads=n_heads)
    return (y_p, y_s, conv_p, k_p, v_p, conv_s, k_s, v_s)
```

```python
import functools
from typing import NamedTuple

import numpy as np
import jax
import jax.numpy as jnp
from jax import lax
from jax.experimental import pallas as pl
from jax.experimental.pallas import tpu as pltpu

CHUNK = 64
BAND_CHUNKS = 8
BAND_PAST = BAND_CHUNKS * CHUNK
MAX_REL = 128
PAST_LEN = 1024
EPS = 1e-6
NEG_INF = -1e30

LANES = 128
HEAD_PAIR = LANES
CONV_HALO = 32
Q_TILE = 4 * CHUNK
TOKEN_TILE = 512
WIDE_TOKEN_TILE = 1024
VMEM_LIMIT = 56 * 1024 * 1024

F32 = jnp.float32
BF16 = jnp.bfloat16


def _rms(x, g):
    ms = jnp.mean(x * x, axis=-1, keepdims=True)
    return x * lax.rsqrt(ms + EPS) * g


def _dot(a, b):
    return jnp.dot(a, b, preferred_element_type=F32)


def _resident(block_shape, index):
    return pl.BlockSpec(block_shape, lambda *_: index, pipeline_mode=pl.Buffered(1))


def _params(*semantics):
    return pltpu.CompilerParams(dimension_semantics=semantics, vmem_limit_bytes=VMEM_LIMIT)


def _ffn_kernel(*refs, d_ff, chunks, has_oproj, has_pe, has_final):
    it = iter(refs)
    x_ref = next(it)
    if has_oproj:
        o_ref, wo_ref = next(it), next(it)
    g_ref, win_ref, wout_ref = next(it), next(it), next(it)
    if has_pe:
        g3_ref, wgate_ref, p_ref, wproj_ref = next(it), next(it), next(it), next(it)
    if has_final:
        gf_ref = next(it)
    out_ref = next(it)

    x = x_ref[...]
    if has_oproj:
        o = jnp.concatenate([o_ref[hp] for hp in range(o_ref.shape[0])], axis=-1)
        x = x + _dot(o, wo_ref[...])
    h = _rms(x, g_ref[...]).astype(BF16)
    acc = None
    for c0, c1 in chunks:
        a = _dot(h, win_ref[:, c0:c1])
        b = _dot(h, win_ref[:, d_ff + c0:d_ff + c1])
        u = (a * jax.nn.sigmoid(a) * b).astype(BF16)
        d = _dot(u, wout_ref[c0:c1, :])
        acc = d if acc is None else acc + d
    x = x + 0.5 * acc
    if has_pe:
        gate = jax.nn.sigmoid(_dot(_rms(x, g3_ref[...]).astype(BF16), wgate_ref[...]))
        x = x + gate * _dot(p_ref[...].astype(BF16), wproj_ref[...])
    if has_final:
        x = _rms(x, gf_ref[...])
    out_ref[...] = x


def _ffn(x, g, w_in, w_out, layer, half, *, tm, oproj=None, pe=None, final_g=None):
    n, d = x.shape
    d_ff = w_out.shape[2]
    step = 512 if d_ff >= 512 else d_ff
    chunks = tuple((c, min(c + step, d_ff)) for c in range(0, d_ff, step))
    row = lambda i: (i, 0)
    args, specs = [x], [pl.BlockSpec((tm, d), row)]
    if oproj is not None:
        o, w_o, li = oproj
        args += [o, w_o]
        specs += [pl.BlockSpec((o.shape[0], tm, HEAD_PAIR), lambda i: (0, i, 0)),
                  _resident((None, d, d), (li, 0, 0))]
    args += [g, w_in, w_out]
    specs += [_resident((1, d), (0, 0)),
              _resident((None, None, d, 2 * d_ff), (layer, half, 0, 0)),
              _resident((None, None, d_ff, d), (layer, half, 0, 0))]
    if pe is not None:
        g3, w_gate, p, w_proj = pe
        pe_dim = p.shape[-1]
        args += [g3, w_gate, p, w_proj]
        specs += [_resident((1, d), (0, 0)),
                  _resident((None, d, d), (layer, 0, 0)),
                  pl.BlockSpec((None, tm, pe_dim), lambda i: (layer, i, 0)),
                  _resident((None, pe_dim, d), (layer, 0, 0))]
    if final_g is not None:
        args += [final_g]
        specs += [_resident((1, d), (0, 0))]
    body = functools.partial(_ffn_kernel, d_ff=d_ff, chunks=chunks, has_oproj=oproj is not None,
                             has_pe=pe is not None, has_final=final_g is not None)
    return pl.pallas_call(
        body, grid=(n // tm,), in_specs=specs, out_specs=pl.BlockSpec((tm, d), row),
        out_shape=jax.ShapeDtypeStruct((n, d), F32), compiler_params=_params("arbitrary"),
        name="ffn")(*args)


def _conv_kernel(x_ref, buf_ref, g1_ref, win_ref, bin_ref, dw_ref, dwb_ref, cg_ref, wout_ref, bout_ref,
                 out_ref, nbuf_ref, ext_ref, *, tm, width, row_block):
    d = x_ref.shape[-1]
    ncb = d // LANES

    @pl.when(pl.program_id(1) == 0)
    def _():
        for cb in range(ncb):
            ext_ref[cb, 0:CONV_HALO, :] = buf_ref[0, :, cb * LANES:(cb + 1) * LANES]

    x = x_ref[0]
    h = _rms(x, g1_ref[...]).astype(BF16)
    lead = CONV_HALO - (width - 1)
    group = 2 * LANES
    cols = []

    def project(c0):
        cs = slice(c0, c0 + group)
        a = _dot(h, win_ref[:, cs]) + bin_ref[:, cs]
        gate = _dot(h, win_ref[:, d + c0:d + c0 + group]) + bin_ref[:, d + c0:d + c0 + group]
        return a * jax.nn.sigmoid(gate)

    glu_next = project(0)
    for c0 in range(0, d, group):
        glu = glu_next
        if c0 + group < d:
            glu_next = project(c0 + group)
        for cb in range(c0 // LANES, (c0 + group) // LANES):
            lanes = slice(cb * LANES, (cb + 1) * LANES)
            ext_ref[cb, CONV_HALO:CONV_HALO + tm, :] = glu[:, cb * LANES - c0:(cb + 1) * LANES - c0]
            blocks = []
            for r0 in range(0, tm, row_block):
                acc = jnp.broadcast_to(dwb_ref[:, lanes], (row_block, LANES))
                for j in range(width):
                    acc = acc + dw_ref[j:j + 1, lanes] * ext_ref[cb, r0 + j + lead:r0 + j + lead + row_block, :]
                blocks.append(acc)
            cols.append(jnp.concatenate(blocks, axis=0) if len(blocks) > 1 else blocks[0])
    y = jnp.concatenate(cols, axis=-1)
    y = _rms(y, cg_ref[...])
    y = (y * jax.nn.sigmoid(y)).astype(BF16)
    out_ref[0] = x + (_dot(y, wout_ref[...]) + bout_ref[...])

    for cb in range(ncb):
        tail = ext_ref[cb, tm:tm + CONV_HALO, :]
        nbuf_ref[0, :, cb * LANES:(cb + 1) * LANES] = tail
        ext_ref[cb, 0:CONV_HALO, :] = tail


def _conv(x, buf, g1, w_in, b_in, dw, dw_b, cg, w_out, b_out, j, *, tm):
    b, s, d = x.shape
    width = dw.shape[1]
    row_block = min(tm, 128)
    tile = lambda bi, ti: (bi, ti, 0)
    specs = [pl.BlockSpec((1, tm, d), tile),
             pl.BlockSpec((1, CONV_HALO, d), lambda bi, ti: (bi, 0, 0)),
             _resident((1, d), (0, 0)),
             _resident((None, d, 2 * d), (j, 0, 0)),
             _resident((None, 1, 2 * d), (j, 0, 0)),
             _resident((None, width, d), (j, 0, 0)),
             _resident((None, 1, d), (j, 0, 0)),
             _resident((None, 1, d), (j, 0, 0)),
             _resident((None, d, d), (j, 0, 0)),
             _resident((None, 1, d), (j, 0, 0))]
    body = functools.partial(_conv_kernel, tm=tm, width=width, row_block=row_block)
    return pl.pallas_call(
        body, grid=(b, s // tm), in_specs=specs,
        out_specs=[pl.BlockSpec((1, tm, d), tile), pl.BlockSpec((1, CONV_HALO, d), lambda bi, ti: (bi, 0, 0))],
        out_shape=[jax.ShapeDtypeStruct((b, s, d), F32), jax.ShapeDtypeStruct((b, CONV_HALO, d), F32)],
        scratch_shapes=[pltpu.VMEM((d // LANES, CONV_HALO + tm, LANES), F32)],
        compiler_params=_params("arbitrary", "arbitrary"), name="conv_module",
    )(x, buf, g1, w_in, b_in, dw, dw_b, cg, w_out, b_out)


def _qkv_kernel(x_ref, g_ref, w_ref, q_ref, k_ref, v_ref, k32_ref, v32_ref, *, scale):
    d = x_ref.shape[-1]
    h = _rms(x_ref[...], g_ref[...]).astype(BF16)
    qkv = _dot(h, w_ref[...])
    for hp in range(d // HEAD_PAIR):
        lanes = slice(hp * HEAD_PAIR, (hp + 1) * HEAD_PAIR)
        q_ref[hp] = (qkv[:, lanes] * scale).astype(BF16)
        k_ref[hp] = qkv[:, d:2 * d][:, lanes].astype(BF16)
        v_ref[hp] = qkv[:, 2 * d:][:, lanes].astype(BF16)
    k32_ref[...] = qkv[:, d:2 * d]
    v32_ref[...] = qkv[:, 2 * d:]


def _qkv(x, g, w_qkv, j, *, tm, tiles_per_tail, scale):
    n, d = x.shape
    nt = n // tm
    hp = d // HEAD_PAIR
    pair_spec = pl.BlockSpec((hp, tm, HEAD_PAIR), lambda i: (0, i, 0))
    tail_spec = pl.BlockSpec((tm, d), lambda i: (i // tiles_per_tail, 0))
    pair_shape = jax.ShapeDtypeStruct((hp, n, HEAD_PAIR), BF16)
    tail_shape = jax.ShapeDtypeStruct((nt // tiles_per_tail * tm, d), F32)
    return pl.pallas_call(
        functools.partial(_qkv_kernel, scale=scale), grid=(nt,),
        in_specs=[pl.BlockSpec((tm, d), lambda i: (i, 0)), _resident((1, d), (0, 0)),
                  _resident((None, d, 3 * d), (j, 0, 0))],
        out_specs=[pair_spec, pair_spec, pair_spec, tail_spec, tail_spec],
        out_shape=[pair_shape, pair_shape, pair_shape, tail_shape, tail_shape],
        compiler_params=_params("arbitrary"), name="qkv_proj")(x, g, w_qkv)


def _own_lanes(hh, head_dim):
    lane = lax.broadcasted_iota(jnp.int32, (1, HEAD_PAIR), 1)
    return (lane < head_dim) if hh == 0 else (lane >= head_dim)


class _FeatureMajor(NamedTuple):
    block: jax.Array


def _dot_nt(a, b):
    return lax.dot_general(a, b, (((1,), (1,)), ((), ())), preferred_element_type=F32)


def _scores(q2, keys, biases, hh, head_dim):
    qh = jnp.where(_own_lanes(hh, head_dim), q2, jnp.zeros_like(q2))
    return [(_dot(qh, k.block) if isinstance(k, _FeatureMajor) else _dot_nt(qh, k)) + bias
            for k, bias in zip(keys, biases)]


def _weighted_values(s, values, hh, head_dim):
    mine = _own_lanes(hh, head_dim)
    if len({sj.shape[1] for sj in s}) == 1:
        m = jnp.max(functools.reduce(jnp.maximum, s), axis=-1, keepdims=True)
    else:
        m = functools.reduce(jnp.maximum, [jnp.max(sj, axis=-1, keepdims=True) for sj in s])

    def weighted(p, v):
        if isinstance(v, _FeatureMajor):
            row = lax.broadcasted_iota(jnp.int32, (HEAD_PAIR, 1), 0)
            mine_rows = (row < head_dim) if hh == 0 else (row >= head_dim)
            return _dot_nt(p, jnp.where(mine_rows, v.block, jnp.ones_like(v.block)))
        return _dot(p, jnp.where(mine, v, jnp.ones_like(v)))

    return functools.reduce(jnp.add, [weighted(jnp.exp(sj - m).astype(BF16), v) for sj, v in zip(s, values)])


def _normalize_pair(acc0, acc1, head_dim):
    low = _own_lanes(0, head_dim)
    num = jnp.where(low, acc0, acc1)
    den = pltpu.roll(jnp.where(low, acc1, acc0), head_dim, 1)
    return num / den


def _attend_heads(n_heads, q_of, keys_of, values_of, biases_of, store, head_dim):
    s_next = _scores(q_of(0), keys_of(0), biases_of(0), 0, head_dim)
    acc = [None, None]
    for h in range(n_heads):
        s_cur = s_next
        if h + 1 < n_heads:
            s_next = _scores(q_of((h + 1) // 2), keys_of((h + 1) // 2), biases_of(h + 1), (h + 1) % 2, head_dim)
        acc[h % 2] = _weighted_values(s_cur, values_of(h // 2), h % 2, head_dim)
        if h % 2 == 1:
            store(h // 2, _normalize_pair(acc[0], acc[1], head_dim))


def _attn_prompt_kernel(q_ref, k0_ref, k1_ref, k2_ref, k3_ref, v0_ref, v1_ref, v2_ref, v3_ref, line_ref, o_ref,
                        bm_ref, *, head_dim):
    u = pl.program_id(1)
    k_refs, v_refs = (k0_ref, k1_ref, k2_ref, k3_ref), (v0_ref, v1_ref, v2_ref, v3_ref)
    n_pairs = q_ref.shape[0]

    @pl.when((pl.program_id(0) == 0) & (u == 0))
    def _():
        _fill_prompt_bias(line_ref, bm_ref)

    def rows(pair):
        tile = pair // n_pairs
        return slice(tile * Q_TILE, (tile + 1) * Q_TILE)

    def store(pair, o):
        o_ref[pair % n_pairs, rows(pair), :] = o.astype(BF16)

    def run(first_step):
        def blocks(pair):
            tile = pair // n_pairs
            return [j for j in (tile, tile + 1, tile + 2) if j >= 2 or not first_step]

        _attend_heads(4 * n_pairs, lambda pair: q_ref[pair % n_pairs, rows(pair), :],
                      lambda pair: [k_refs[j][pair % n_pairs] for j in blocks(pair)],
                      lambda pair: [v_refs[j][pair % n_pairs] for j in blocks(pair)],
                      lambda h: [bm_ref[h % (2 * n_pairs), j - h // (2 * n_pairs)] for j in blocks(h // 2)],
                      store, head_dim)

    pl.when(u == 0)(lambda: run(True))
    pl.when(u > 0)(lambda: run(False))


def _bias_line(table):
    rows, cols = Q_TILE, 3 * Q_TILE
    u = np.arange(rows + cols)
    u = np.where(u < cols, u, u - (rows + cols))
    slots = np.clip(2 * Q_TILE - u, -MAX_REL, MAX_REL) + MAX_REL
    return table[:, slots][:, None, :]


def _fill_prompt_bias(line_ref, bm_ref):
    n_heads, rows = bm_ref.shape[0], bm_ref.shape[2]
    width = line_ref.shape[-1]
    cols = bm_ref.shape[1] * rows
    qc = lax.broadcasted_iota(jnp.int32, (rows, cols), 0) // CHUNK
    kc = lax.broadcasted_iota(jnp.int32, (rows, cols), 1) // CHUNK - (cols - rows) // CHUNK
    band = (kc <= qc) & (kc >= qc - BAND_CHUNKS)
    for h in range(n_heads):
        rotated = pltpu.roll(jnp.broadcast_to(line_ref[h], (rows, width)), 0, 1, stride=1, stride_axis=0)
        bias = jnp.where(band, rotated[:, :cols], NEG_INF)
        for j in range(bm_ref.shape[1]):
            bm_ref[h, j] = bias[:, j * rows:(j + 1) * rows]


def _attn_prompt(q, k, v, table, b, s):
    hp, n, _ = q.shape
    nq = s // Q_TILE
    steps = nq // 2
    head_dim = HEAD_PAIR // 2
    line = _bias_line(table)
    n_heads = table.shape[0]

    def key_spec(j):
        return pl.BlockSpec((hp, Q_TILE, HEAD_PAIR),
                            lambda bi, ui: (0, bi * nq + jnp.maximum(2 * ui - 2 + j, 0), 0))

    tile_spec = pl.BlockSpec((hp, 2 * Q_TILE, HEAD_PAIR), lambda bi, ui: (0, bi * steps + ui, 0))
    keys = [key_spec(j) for j in range(4)]
    return pl.pallas_call(
        functools.partial(_attn_prompt_kernel, head_dim=head_dim), grid=(b, steps),
        in_specs=[tile_spec] + keys + keys + [_resident(line.shape, (0, 0, 0))],
        out_specs=tile_spec, out_shape=jax.ShapeDtypeStruct((hp, n, HEAD_PAIR), BF16),
        scratch_shapes=[pltpu.VMEM((n_heads, 3, Q_TILE, Q_TILE), F32)],
        compiler_params=_params("arbitrary", "arbitrary"), name="attn_prompt")(q, k, k, k, k, v, v, v, v, line)


def _attn_sample_kernel(q_ref, kn_ref, vn_ref, ck_ref, cv_ref, bc_ref, bn_ref, o_ref, *, head_dim):
    def lanes(hp):
        return slice(hp * HEAD_PAIR, (hp + 1) * HEAD_PAIR)

    def store(hp, o):
        o_ref[hp] = o.astype(BF16)

    _attend_heads(2 * q_ref.shape[0], lambda hp: q_ref[hp],
                  lambda hp: [_FeatureMajor(ck_ref[lanes(hp), :].astype(BF16)), kn_ref[hp]],
                  lambda hp: [_FeatureMajor(cv_ref[lanes(hp), :].astype(BF16)), vn_ref[hp]],
                  lambda h: [bc_ref[h], bn_ref[h]], store, head_dim)


def _attn_sample(q, k, v, ck, cv, j, table, b, t):
    hp, n, _ = q.shape
    d, w = ck.shape[2], ck.shape[3]
    head_dim = HEAD_PAIR // 2
    qpos = PAST_LEN + np.arange(t)
    kpos = np.concatenate([PAST_LEN - w + np.arange(w), qpos])
    rel = np.clip(qpos[:, None] - kpos[None, :], -MAX_REL, MAX_REL) + MAX_REL
    qc, kc = qpos[:, None] // CHUNK, kpos[None, :] // CHUNK
    mask = (kpos[None, :] >= 0) & (kc <= qc) & (kc >= qc - BAND_CHUNKS)
    bias = jnp.where(mask[None], table[:, rel], NEG_INF)
    new_spec = pl.BlockSpec((hp, t, HEAD_PAIR), lambda bi: (0, bi, 0))
    cache_spec = pl.BlockSpec((None, None, d, w), lambda bi: (j, bi, 0, 0))
    return pl.pallas_call(
        functools.partial(_attn_sample_kernel, head_dim=head_dim), grid=(b,),
        in_specs=[new_spec, new_spec, new_spec, cache_spec, cache_spec,
                  _resident((bias.shape[0], t, w), (0, 0, 0)), _resident((bias.shape[0], t, t), (0, 0, 0))],
        out_specs=new_spec, out_shape=jax.ShapeDtypeStruct((hp, n, HEAD_PAIR), BF16),
        compiler_params=_params("arbitrary"), name="attn_sample",
    )(q, k, v, ck, cv, bias[:, :, :w], bias[:, :, w:])


def _trunk(x, p, conv_state, cache_k, cache_v, w, *, tm, tm_wide, n_heads):
    b, s, d = x.shape
    depth = w["norm_g"].shape[0]
    n = b * s
    is_prompt = conv_state is None
    head_dim = d // n_heads
    width = w["conv_dw"].shape[1]
    p = p.reshape(depth, n, p.shape[-1])
    g = lambda i, k: w["norm_g"][i, k].reshape(1, d)
    x = x.reshape(n, d)
    new_conv, new_k, new_v = [], [], []
    oproj = None
    for i in range(depth):
        j = i // 2
        x = _ffn(x, g(i, 0), w["ffn_w_in"], w["ffn_w_out"], i, 0, tm=tm_wide)
        if i % 2 == 0:
            if is_prompt:
                buf = jnp.zeros((b, CONV_HALO, d), F32)
            else:
                buf = jnp.pad(conv_state[j], ((0, 0), (CONV_HALO - (width - 1), 0), (0, 0)))
            x, nbuf = _conv(x.reshape(b, s, d), buf, g(i, 1), w["conv_w_in"], w["conv_b_in"], w["conv_dw"],
                            w["conv_dw_b"], w["conv_norm_g"], w["conv_w_out"], w["conv_b_out"], j,
                            tm=min(tm_wide, s))
            x = x.reshape(n, d)
            new_conv.append(nbuf[:, CONV_HALO - (width - 1):])
            oproj = None
        else:
            keep = min(BAND_PAST, s) if is_prompt else s
            q, k, v, k32, v32 = _qkv(x, g(i, 1), w["attn_w_qkv"], j, tm=tm,
                                     tiles_per_tail=(s // keep) if is_prompt else 1, scale=head_dim ** -0.5)
            if is_prompt:
                o = _attn_prompt(q, k, v, w["attn_rel_table"][j], b, s)
            else:
                o = _attn_sample(q, k, v, cache_k, cache_v, j, w["attn_rel_table"][j], b, s)
            new_k.append(k32.reshape(b, keep, n_heads, head_dim))
            new_v.append(v32.reshape(b, keep, n_heads, head_dim))
            oproj = (o, w["attn_w_o"], j)
        x = _ffn(x, g(i, 2), w["ffn_w_in"], w["ffn_w_out"], i, 1, tm=tm, oproj=oproj,
                 pe=(g(i, 3), w["pe_w_gate"], p, w["pe_w_proj"]),
                 final_g=w["final_g"].reshape(1, d) if i == depth - 1 else None)
    return x.reshape(b, s, d), jnp.stack(new_conv), jnp.stack(new_k), jnp.stack(new_v)


def kernel(x_prompt, x_sample, cache_conv, cache_k, cache_v, p_prompt, p_sample, norm_g, final_g, ffn_w_in,
           ffn_w_out, conv_w_in, conv_b_in, conv_dw, conv_dw_b, conv_norm_g, conv_w_out, conv_b_out,
           attn_w_qkv, attn_w_o, attn_rel_table, pe_w_proj, pe_w_gate):
    n_conv = conv_w_in.shape[0]
    w = dict(norm_g=norm_g, final_g=final_g,
             ffn_w_in=ffn_w_in.astype(BF16), ffn_w_out=ffn_w_out.astype(BF16),
             conv_w_in=conv_w_in.astype(BF16), conv_b_in=conv_b_in.reshape(n_conv, 1, -1), conv_dw=conv_dw,
             conv_dw_b=conv_dw_b.reshape(n_conv, 1, -1), conv_norm_g=conv_norm_g.reshape(n_conv, 1, -1),
             conv_w_out=conv_w_out.astype(BF16), conv_b_out=conv_b_out.reshape(n_conv, 1, -1),
             attn_w_qkv=attn_w_qkv.astype(BF16), attn_w_o=attn_w_o.astype(BF16), attn_rel_table=attn_rel_table,
             pe_w_proj=pe_w_proj.astype(BF16), pe_w_gate=pe_w_gate.astype(BF16))
    n_sample = x_sample.shape[0] * x_sample.shape[1]
    n_attn, n_streams, window, n_heads, head_dim = cache_k.shape
    ck = jnp.transpose(cache_k, (0, 1, 3, 4, 2)).reshape(n_attn, n_streams, n_heads * head_dim, window)
    cv = jnp.transpose(cache_v, (0, 1, 3, 4, 2)).reshape(n_attn, n_streams, n_heads * head_dim, window)
    y_p, conv_p, k_p, v_p = _trunk(x_prompt, p_prompt, None, None, None, w, tm=TOKEN_TILE, tm_wide=WIDE_TOKEN_TILE,
                                   n_heads=n_heads)
    tm_s = min(TOKEN_TILE, n_sample)
    y_s, conv_s, k_s, v_s = _trunk(x_sample, p_sample, cache_conv, ck, cv, w, tm=tm_s, tm_wide=tm_s, n_heads=n_heads)
    return (y_p, y_s, conv_p, k_p, v_p, conv_s, k_s, v_s)
```

```python
import functools
from typing import NamedTuple

import numpy as np
import jax
import jax.numpy as jnp
from jax import lax
from jax.experimental import pallas as pl
from jax.experimental.pallas import tpu as pltpu

CHUNK = 64
BAND_CHUNKS = 8
BAND_PAST = BAND_CHUNKS * CHUNK
MAX_REL = 128
PAST_LEN = 1024
EPS = 1e-6
NEG_INF = -1e30

LANES = 128
HEAD_PAIR = LANES
CONV_HALO = 32
Q_TILE = 4 * CHUNK
TOKEN_TILE = 512
WIDE_TOKEN_TILE = 1024
VMEM_LIMIT = 56 * 1024 * 1024

F32 = jnp.float32
BF16 = jnp.bfloat16


def _rms(x, g):
    ms = jnp.mean(x * x, axis=-1, keepdims=True)
    return x * lax.rsqrt(ms + EPS) * g


def _dot(a, b):
    return jnp.dot(a, b, preferred_element_type=F32)


def _resident(block_shape, index):
    return pl.BlockSpec(block_shape, lambda *_: index, pipeline_mode=pl.Buffered(1))


def _params(*semantics):
    return pltpu.CompilerParams(dimension_semantics=semantics, vmem_limit_bytes=VMEM_LIMIT)


def _ffn_kernel(*refs, d_ff, chunks, has_oproj, has_pe, has_final):
    it = iter(refs)
    x_ref = next(it)
    if has_oproj:
        o_ref, wo_ref = next(it), next(it)
    g_ref, win_ref, wout_ref = next(it), next(it), next(it)
    if has_pe:
        g3_ref, wgate_ref, p_ref, wproj_ref = next(it), next(it), next(it), next(it)
    if has_final:
        gf_ref = next(it)
    out_ref = next(it)

    x = x_ref[...]
    if has_oproj:
        o = jnp.concatenate([o_ref[hp] for hp in range(o_ref.shape[0])], axis=-1)
        x = x + _dot(o, wo_ref[...])
    h = _rms(x, g_ref[...]).astype(BF16)
    acc = None
    for c0, c1 in chunks:
        a = _dot(h, win_ref[:, c0:c1])
        b = _dot(h, win_ref[:, d_ff + c0:d_ff + c1])
        u = (a * jax.nn.sigmoid(a) * b).astype(BF16)
        d = _dot(u, wout_ref[c0:c1, :])
        acc = d if acc is None else acc + d
    x = x + 0.5 * acc
    if has_pe:
        gate = jax.nn.sigmoid(_dot(_rms(x, g3_ref[...]).astype(BF16), wgate_ref[...]))
        x = x + gate * _dot(p_ref[...].astype(BF16), wproj_ref[...])
    if has_final:
        x = _rms(x, gf_ref[...])
    out_ref[...] = x


def _ffn(x, g, w_in, w_out, layer, half, *, tm, oproj=None, pe=None, final_g=None):
    n, d = x.shape
    d_ff = w_out.shape[2]
    step = 256 if d_ff >= 256 else d_ff
    chunks = tuple((c, min(c + step, d_ff)) for c in range(0, d_ff, step))
    row = lambda i: (i, 0)
    args, specs = [x], [pl.BlockSpec((tm, d), row)]
    if oproj is not None:
        o, w_o, li = oproj
        args += [o, w_o]
        specs += [pl.BlockSpec((o.shape[0], tm, HEAD_PAIR), lambda i: (0, i, 0)),
                  _resident((None, d, d), (li, 0, 0))]
    args += [g, w_in, w_out]
    specs += [_resident((1, d), (0, 0)),
              _resident((None, None, d, 2 * d_ff), (layer, half, 0, 0)),
              _resident((None, None, d_ff, d), (layer, half, 0, 0))]
    if pe is not None:
        g3, w_gate, p, w_proj = pe
        pe_dim = p.shape[-1]
        args += [g3, w_gate, p, w_proj]
        specs += [_resident((1, d), (0, 0)),
                  _resident((None, d, d), (layer, 0, 0)),
                  pl.BlockSpec((None, tm, pe_dim), lambda i: (layer, i, 0)),
                  _resident((None, pe_dim, d), (layer, 0, 0))]
    if final_g is not None:
        args += [final_g]
        specs += [_resident((1, d), (0, 0))]
    body = functools.partial(_ffn_kernel, d_ff=d_ff, chunks=chunks, has_oproj=oproj is not None,
                             has_pe=pe is not None, has_final=final_g is not None)
    return pl.pallas_call(
        body, grid=(n // tm,), in_specs=specs, out_specs=pl.BlockSpec((tm, d), row),
        out_shape=jax.ShapeDtypeStruct((n, d), F32), compiler_params=_params("arbitrary"),
        name="ffn")(*args)


def _conv_kernel(x_ref, buf_ref, g1_ref, win_ref, bin_ref, dw_ref, dwb_ref, cg_ref, wout_ref, bout_ref,
                 out_ref, nbuf_ref, ext_ref, *, tm, width, row_block):
    d = x_ref.shape[-1]
    ncb = d // LANES

    @pl.when(pl.program_id(1) == 0)
    def _():
        for cb in range(ncb):
            ext_ref[cb, 0:CONV_HALO, :] = buf_ref[0, :, cb * LANES:(cb + 1) * LANES]

    x = x_ref[0]
    h = _rms(x, g1_ref[...]).astype(BF16)
    lead = CONV_HALO - (width - 1)
    group = 2 * LANES
    cols = []

    def project(c0):
        cs = slice(c0, c0 + group)
        a = _dot(h, win_ref[:, cs]) + bin_ref[:, cs]
        gate = _dot(h, win_ref[:, d + c0:d + c0 + group]) + bin_ref[:, d + c0:d + c0 + group]
        return a * jax.nn.sigmoid(gate)

    glu_next = project(0)
    for c0 in range(0, d, group):
        glu = glu_next
        if c0 + group < d:
            glu_next = project(c0 + group)
        for cb in range(c0 // LANES, (c0 + group) // LANES):
            lanes = slice(cb * LANES, (cb + 1) * LANES)
            ext_ref[cb, CONV_HALO:CONV_HALO + tm, :] = glu[:, cb * LANES - c0:(cb + 1) * LANES - c0]
            blocks = []
            for r0 in range(0, tm, row_block):
                acc = jnp.broadcast_to(dwb_ref[:, lanes], (row_block, LANES))
                for j in range(width):
                    acc = acc + dw_ref[j:j + 1, lanes] * ext_ref[cb, r0 + j + lead:r0 + j + lead + row_block, :]
                blocks.append(acc)
            cols.append(jnp.concatenate(blocks, axis=0) if len(blocks) > 1 else blocks[0])
    y = jnp.concatenate(cols, axis=-1)
    y = _rms(y, cg_ref[...])
    y = (y * jax.nn.sigmoid(y)).astype(BF16)
    out_ref[0] = x + (_dot(y, wout_ref[...]) + bout_ref[...])

    for cb in range(ncb):
        tail = ext_ref[cb, tm:tm + CONV_HALO, :]
        nbuf_ref[0, :, cb * LANES:(cb + 1) * LANES] = tail
        ext_ref[cb, 0:CONV_HALO, :] = tail


def _conv(x, buf, g1, w_in, b_in, dw, dw_b, cg, w_out, b_out, j, *, tm):
    b, s, d = x.shape
    width = dw.shape[1]
    row_block = min(tm, 128)
    tile = lambda bi, ti: (bi, ti, 0)
    specs = [pl.BlockSpec((1, tm, d), tile),
             pl.BlockSpec((1, CONV_HALO, d), lambda bi, ti: (bi, 0, 0)),
             _resident((1, d), (0, 0)),
             _resident((None, d, 2 * d), (j, 0, 0)),
             _resident((None, 1, 2 * d), (j, 0, 0)),
             _resident((None, width, d), (j, 0, 0)),
             _resident((None, 1, d), (j, 0, 0)),
             _resident((None, 1, d), (j, 0, 0)),
             _resident((None, d, d), (j, 0, 0)),
             _resident((None, 1, d), (j, 0, 0))]
    body = functools.partial(_conv_kernel, tm=tm, width=width, row_block=row_block)
    return pl.pallas_call(
        body, grid=(b, s // tm), in_specs=specs,
        out_specs=[pl.BlockSpec((1, tm, d), tile), pl.BlockSpec((1, CONV_HALO, d), lambda bi, ti: (bi, 0, 0))],
        out_shape=[jax.ShapeDtypeStruct((b, s, d), F32), jax.ShapeDtypeStruct((b, CONV_HALO, d), F32)],
        scratch_shapes=[pltpu.VMEM((d // LANES, CONV_HALO + tm, LANES), F32)],
        compiler_params=_params("arbitrary", "arbitrary"), name="conv_module",
    )(x, buf, g1, w_in, b_in, dw, dw_b, cg, w_out, b_out)


def _qkv_kernel(x_ref, g_ref, w_ref, q_ref, k_ref, v_ref, k32_ref, v32_ref, *, scale):
    d = x_ref.shape[-1]
    h = _rms(x_ref[...], g_ref[...]).astype(BF16)
    qkv = _dot(h, w_ref[...])
    for hp in range(d // HEAD_PAIR):
        lanes = slice(hp * HEAD_PAIR, (hp + 1) * HEAD_PAIR)
        q_ref[hp] = (qkv[:, lanes] * scale).astype(BF16)
        k_ref[hp] = qkv[:, d:2 * d][:, lanes].astype(BF16)
        v_ref[hp] = qkv[:, 2 * d:][:, lanes].astype(BF16)
    k32_ref[...] = qkv[:, d:2 * d]
    v32_ref[...] = qkv[:, 2 * d:]


def _qkv(x, g, w_qkv, j, *, tm, tiles_per_tail, scale):
    n, d = x.shape
    nt = n // tm
    hp = d // HEAD_PAIR
    pair_spec = pl.BlockSpec((hp, tm, HEAD_PAIR), lambda i: (0, i, 0))
    tail_spec = pl.BlockSpec((tm, d), lambda i: (i // tiles_per_tail, 0))
    pair_shape = jax.ShapeDtypeStruct((hp, n, HEAD_PAIR), BF16)
    tail_shape = jax.ShapeDtypeStruct((nt // tiles_per_tail * tm, d), F32)
    return pl.pallas_call(
        functools.partial(_qkv_kernel, scale=scale), grid=(nt,),
        in_specs=[pl.BlockSpec((tm, d), lambda i: (i, 0)), _resident((1, d), (0, 0)),
                  _resident((None, d, 3 * d), (j, 0, 0))],
        out_specs=[pair_spec, pair_spec, pair_spec, tail_spec, tail_spec],
        out_shape=[pair_shape, pair_shape, pair_shape, tail_shape, tail_shape],
        compiler_params=_params("arbitrary"), name="qkv_proj")(x, g, w_qkv)


def _own_lanes(hh, head_dim):
    lane = lax.broadcasted_iota(jnp.int32, (1, HEAD_PAIR), 1)
    return (lane < head_dim) if hh == 0 else (lane >= head_dim)


class _FeatureMajor(NamedTuple):
    block: jax.Array


def _dot_nt(a, b):
    return lax.dot_general(a, b, (((1,), (1,)), ((), ())), preferred_element_type=F32)


def _scores(q2, keys, biases, hh, head_dim):
    qh = jnp.where(_own_lanes(hh, head_dim), q2, jnp.zeros_like(q2))
    return [(_dot(qh, k.block) if isinstance(k, _FeatureMajor) else _dot_nt(qh, k)) + bias
            for k, bias in zip(keys, biases)]


def _weighted_values(s, values, hh, head_dim):
    mine = _own_lanes(hh, head_dim)
    if len({sj.shape[1] for sj in s}) == 1:
        m = jnp.max(functools.reduce(jnp.maximum, s), axis=-1, keepdims=True)
    else:
        m = functools.reduce(jnp.maximum, [jnp.max(sj, axis=-1, keepdims=True) for sj in s])

    def weighted(p, v):
        if isinstance(v, _FeatureMajor):
            row = lax.broadcasted_iota(jnp.int32, (HEAD_PAIR, 1), 0)
            mine_rows = (row < head_dim) if hh == 0 else (row >= head_dim)
            return _dot_nt(p, jnp.where(mine_rows, v.block, jnp.ones_like(v.block)))
        return _dot(p, jnp.where(mine, v, jnp.ones_like(v)))

    return functools.reduce(jnp.add, [weighted(jnp.exp(sj - m).astype(BF16), v) for sj, v in zip(s, values)])


def _normalize_pair(acc0, acc1, head_dim):
    low = _own_lanes(0, head_dim)
    num = jnp.where(low, acc0, acc1)
    den = pltpu.roll(jnp.where(low, acc1, acc0), head_dim, 1)
    return num / den


def _attend_heads(n_heads, q_of, keys_of, values_of, biases_of, store, head_dim):
    s_next = _scores(q_of(0), keys_of(0), biases_of(0), 0, head_dim)
    acc = [None, None]
    for h in range(n_heads):
        s_cur = s_next
        if h + 1 < n_heads:
            s_next = _scores(q_of((h + 1) // 2), keys_of((h + 1) // 2), biases_of(h + 1), (h + 1) % 2, head_dim)
        acc[h % 2] = _weighted_values(s_cur, values_of(h // 2), h % 2, head_dim)
        if h % 2 == 1:
            store(h // 2, _normalize_pair(acc[0], acc[1], head_dim))


def _attn_prompt_kernel(q_ref, k0_ref, k1_ref, k2_ref, k3_ref, v0_ref, v1_ref, v2_ref, v3_ref, line_ref, o_ref,
                        bm_ref, *, head_dim):
    u = pl.program_id(1)
    k_refs, v_refs = (k0_ref, k1_ref, k2_ref, k3_ref), (v0_ref, v1_ref, v2_ref, v3_ref)
    n_pairs = q_ref.shape[0]

    @pl.when((pl.program_id(0) == 0) & (u == 0))
    def _():
        _fill_prompt_bias(line_ref, bm_ref)

    def rows(pair):
        tile = pair // n_pairs
        return slice(tile * Q_TILE, (tile + 1) * Q_TILE)

    def store(pair, o):
        o_ref[pair % n_pairs, rows(pair), :] = o.astype(BF16)

    def run(first_step):
        def blocks(pair):
            tile = pair // n_pairs
            return [j for j in (tile, tile + 1, tile + 2) if j >= 2 or not first_step]

        _attend_heads(4 * n_pairs, lambda pair: q_ref[pair % n_pairs, rows(pair), :],
                      lambda pair: [k_refs[j][pair % n_pairs] for j in blocks(pair)],
                      lambda pair: [v_refs[j][pair % n_pairs] for j in blocks(pair)],
                      lambda h: [bm_ref[h % (2 * n_pairs), j - h // (2 * n_pairs)] for j in blocks(h // 2)],
                      store, head_dim)

    pl.when(u == 0)(lambda: run(True))
    pl.when(u > 0)(lambda: run(False))


def _bias_line(table):
    rows, cols = Q_TILE, 3 * Q_TILE
    u = np.arange(rows + cols)
    u = np.where(u < cols, u, u - (rows + cols))
    slots = np.clip(2 * Q_TILE - u, -MAX_REL, MAX_REL) + MAX_REL
    return table[:, slots][:, None, :]


def _fill_prompt_bias(line_ref, bm_ref):
    n_heads, rows = bm_ref.shape[0], bm_ref.shape[2]
    width = line_ref.shape[-1]
    cols = bm_ref.shape[1] * rows
    qc = lax.broadcasted_iota(jnp.int32, (rows, cols), 0) // CHUNK
    kc = lax.broadcasted_iota(jnp.int32, (rows, cols), 1) // CHUNK - (cols - rows) // CHUNK
    band = (kc <= qc) & (kc >= qc - BAND_CHUNKS)
    for h in range(n_heads):
        rotated = pltpu.roll(jnp.broadcast_to(line_ref[h], (rows, width)), 0, 1, stride=1, stride_axis=0)
        bias = jnp.where(band, rotated[:, :cols], NEG_INF)
        for j in range(bm_ref.shape[1]):
            bm_ref[h, j] = bias[:, j * rows:(j + 1) * rows]


def _attn_prompt(q, k, v, table, b, s):
    hp, n, _ = q.shape
    nq = s // Q_TILE
    steps = nq // 2
    head_dim = HEAD_PAIR // 2
    line = _bias_line(table)
    n_heads = table.shape[0]

    def key_spec(j):
        return pl.BlockSpec((hp, Q_TILE, HEAD_PAIR),
                            lambda bi, ui: (0, bi * nq + jnp.maximum(2 * ui - 2 + j, 0), 0))

    tile_spec = pl.BlockSpec((hp, 2 * Q_TILE, HEAD_PAIR), lambda bi, ui: (0, bi * steps + ui, 0))
    keys = [key_spec(j) for j in range(4)]
    return pl.pallas_call(
        functools.partial(_attn_prompt_kernel, head_dim=head_dim), grid=(b, steps),
        in_specs=[tile_spec] + keys + keys + [_resident(line.shape, (0, 0, 0))],
        out_specs=tile_spec, out_shape=jax.ShapeDtypeStruct((hp, n, HEAD_PAIR), BF16),
        scratch_shapes=[pltpu.VMEM((n_heads, 3, Q_TILE, Q_TILE), F32)],
        compiler_params=_params("arbitrary", "arbitrary"), name="attn_prompt")(q, k, k, k, k, v, v, v, v, line)


def _attn_sample_kernel(q_ref, kn_ref, vn_ref, ck_ref, cv_ref, bc_ref, bn_ref, o_ref, *, head_dim):
    def lanes(hp):
        return slice(hp * HEAD_PAIR, (hp + 1) * HEAD_PAIR)

    def store(hp, o):
        o_ref[hp] = o.astype(BF16)

    _attend_heads(2 * q_ref.shape[0], lambda hp: q_ref[hp],
                  lambda hp: [_FeatureMajor(ck_ref[lanes(hp), :].astype(BF16)), kn_ref[hp]],
                  lambda hp: [_FeatureMajor(cv_ref[lanes(hp), :].astype(BF16)), vn_ref[hp]],
                  lambda h: [bc_ref[h], bn_ref[h]], store, head_dim)


def _attn_sample(q, k, v, ck, cv, j, table, b, t):
    hp, n, _ = q.shape
    d, w = ck.shape[2], ck.shape[3]
    head_dim = HEAD_PAIR // 2
    qpos = PAST_LEN + np.arange(t)
    kpos = np.concatenate([PAST_LEN - w + np.arange(w), qpos])
    rel = np.clip(qpos[:, None] - kpos[None, :], -MAX_REL, MAX_REL) + MAX_REL
    qc, kc = qpos[:, None] // CHUNK, kpos[None, :] // CHUNK
    mask = (kpos[None, :] >= 0) & (kc <= qc) & (kc >= qc - BAND_CHUNKS)
    bias = jnp.where(mask[None], table[:, rel], NEG_INF)
    new_spec = pl.BlockSpec((hp, t, HEAD_PAIR), lambda bi: (0, bi, 0))
    cache_spec = pl.BlockSpec((None, None, d, w), lambda bi: (j, bi, 0, 0))
    return pl.pallas_call(
        functools.partial(_attn_sample_kernel, head_dim=head_dim), grid=(b,),
        in_specs=[new_spec, new_spec, new_spec, cache_spec, cache_spec,
                  _resident((bias.shape[0], t, w), (0, 0, 0)), _resident((bias.shape[0], t, t), (0, 0, 0))],
        out_specs=new_spec, out_shape=jax.ShapeDtypeStruct((hp, n, HEAD_PAIR), BF16),
        compiler_params=_params("arbitrary"), name="attn_sample",
    )(q, k, v, ck, cv, bias[:, :, :w], bias[:, :, w:])


def _trunk(x, p, conv_state, cache_k, cache_v, w, *, tm, tm_wide, n_heads):
    b, s, d = x.shape
    depth = w["norm_g"].shape[0]
    n = b * s
    is_prompt = conv_state is None
    head_dim = d // n_heads
    width = w["conv_dw"].shape[1]
    p = p.reshape(depth, n, p.shape[-1])
    g = lambda i, k: w["norm_g"][i, k].reshape(1, d)
    x = x.reshape(n, d)
    new_conv, new_k, new_v = [], [], []
    oproj = None
    for i in range(depth):
        j = i // 2
        x = _ffn(x, g(i, 0), w["ffn_w_in"], w["ffn_w_out"], i, 0, tm=tm_wide)
        if i % 2 == 0:
            if is_prompt:
                buf = jnp.zeros((b, CONV_HALO, d), F32)
            else:
                buf = jnp.pad(conv_state[j], ((0, 0), (CONV_HALO - (width - 1), 0), (0, 0)))
            x, nbuf = _conv(x.reshape(b, s, d), buf, g(i, 1), w["conv_w_in"], w["conv_b_in"], w["conv_dw"],
                            w["conv_dw_b"], w["conv_norm_g"], w["conv_w_out"], w["conv_b_out"], j,
                            tm=min(tm_wide, s))
            x = x.reshape(n, d)
            new_conv.append(nbuf[:, CONV_HALO - (width - 1):])
            oproj = None
        else:
            keep = min(BAND_PAST, s) if is_prompt else s
            q, k, v, k32, v32 = _qkv(x, g(i, 1), w["attn_w_qkv"], j, tm=tm,
                                     tiles_per_tail=(s // keep) if is_prompt else 1, scale=head_dim ** -0.5)
            if is_prompt:
                o = _attn_prompt(q, k, v, w["attn_rel_table"][j], b, s)
            else:
                o = _attn_sample(q, k, v, cache_k, cache_v, j, w["attn_rel_table"][j], b, s)
            new_k.append(k32.reshape(b, keep, n_heads, head_dim))
            new_v.append(v32.reshape(b, keep, n_heads, head_dim))
            oproj = (o, w["attn_w_o"], j)
        x = _ffn(x, g(i, 2), w["ffn_w_in"], w["ffn_w_out"], i, 1, tm=tm, oproj=oproj,
                 pe=(g(i, 3), w["pe_w_gate"], p, w["pe_w_proj"]),
                 final_g=w["final_g"].reshape(1, d) if i == depth - 1 else None)
    return x.reshape(b, s, d), jnp.stack(new_conv), jnp.stack(new_k), jnp.stack(new_v)


def kernel(x_prompt, x_sample, cache_conv, cache_k, cache_v, p_prompt, p_sample, norm_g, final_g, ffn_w_in,
           ffn_w_out, conv_w_in, conv_b_in, conv_dw, conv_dw_b, conv_norm_g, conv_w_out, conv_b_out,
           attn_w_qkv, attn_w_o, attn_rel_table, pe_w_proj, pe_w_gate):
    n_conv = conv_w_in.shape[0]
    w = dict(norm_g=norm_g, final_g=final_g,
             ffn_w_in=ffn_w_in.astype(BF16), ffn_w_out=ffn_w_out.astype(BF16),
             conv_w_in=conv_w_in.astype(BF16), conv_b_in=conv_b_in.reshape(n_conv, 1, -1), conv_dw=conv_dw,
             conv_dw_b=conv_dw_b.reshape(n_conv, 1, -1), conv_norm_g=conv_norm_g.reshape(n_conv, 1, -1),
             conv_w_out=conv_w_out.astype(BF16), conv_b_out=conv_b_out.reshape(n_conv, 1, -1),
             attn_w_qkv=attn_w_qkv.astype(BF16), attn_w_o=attn_w_o.astype(BF16), attn_rel_table=attn_rel_table,
             pe_w_proj=pe_w_proj.astype(BF16), pe_w_gate=pe_w_gate.astype(BF16))
    n_sample = x_sample.shape[0] * x_sample.shape[1]
    n_attn, n_streams, window, n_heads, head_dim = cache_k.shape
    ck = jnp.transpose(cache_k, (0, 1, 3, 4, 2)).reshape(n_attn, n_streams, n_heads * head_dim, window)
    cv = jnp.transpose(cache_v, (0, 1, 3, 4, 2)).reshape(n_attn, n_streams, n_heads * head_dim, window)
    y_p, conv_p, k_p, v_p = _trunk(x_prompt, p_prompt, None, None, None, w, tm=TOKEN_TILE, tm_wide=WIDE_TOKEN_TILE,
                                   n_heads=n_heads)
    tm_s = min(TOKEN_TILE, n_sample)
    y_s, conv_s, k_s, v_s = _trunk(x_sample, p_sample, cache_conv, ck, cv, w, tm=tm_s, tm_wide=tm_s, n_heads=n_heads)
    return (y_p, y_s, conv_p, k_p, v_p, conv_s, k_s, v_s)
```
